```python
import jax, jax.numpy as jnp
from jax import lax
import numpy as np

D_MODEL = 4096
BATCH = 8
SEQ = 2048
DEPTH = 2
DEC_BATCH = 1
DEC_SEQ = 16384
PAST_LEN = 128

N_MIXERS = 2
N_ATTN_LAYERS = (DEPTH + 1) // 2
N_RET_LAYERS = DEPTH // 2
ATT_HEAD_DIM = 128
ATT_HEADS = D_MODEL // ATT_HEAD_DIM
ATT_KV_HEADS = ATT_HEADS // 4
ATT_GROUP = ATT_HEADS // ATT_KV_HEADS
ATT_WINDOW = 128
ATT_BLOCK = 128
ATT_IN = (ATT_HEADS + 2 * ATT_KV_HEADS) * ATT_HEAD_DIM
RET_HEADS = 16
RET_DK = D_MODEL // RET_HEADS
RET_DV = 2 * D_MODEL // RET_HEADS
RET_CHUNK = 128
RET_IN = 2 * RET_HEADS * RET_DK + 2 * RET_HEADS * RET_DV
N_EXPERTS = 16
EC_CAPACITY = 2
EXPERT_FF = D_MODEL
NORM_EPS = 1e-6
NEG_INF = -1e30

kernel_name = "hybrid_swa_sink_retention_ec_encoder"


def _rmsnorm(x, g):
    xf = x.astype(jnp.float32)
    y = xf * lax.rsqrt(jnp.mean(xf * xf, axis=-1, keepdims=True) + NORM_EPS)
    return y.astype(x.dtype) * g


def _alibi_slopes(n_heads):
    return jnp.exp2(-8.0 * jnp.arange(1, n_heads + 1, dtype=jnp.float32) / n_heads)


def _band_blocks(a, n_blocks):
    B, T, c, d = a.shape
    ap = jnp.pad(a, ((0, 0), (ATT_BLOCK, ATT_BLOCK), (0, 0), (0, 0)))
    ap = ap.reshape(B, n_blocks + 2, ATT_BLOCK, c, d)
    return jnp.concatenate([ap[:, :-2], ap[:, 1:-1], ap[:, 2:]], axis=2)


def _window_attention(x, w_in, w_out, sink):
    B, T, _ = x.shape
    nb = T // ATT_BLOCK
    qd = ATT_HEADS * ATT_HEAD_DIM
    kd = ATT_KV_HEADS * ATT_HEAD_DIM
    proj = x @ w_in
    q = proj[..., :qd].reshape(B, nb, ATT_BLOCK, ATT_KV_HEADS, ATT_GROUP, ATT_HEAD_DIM)
    k = proj[..., qd:qd + kd].reshape(B, T, ATT_KV_HEADS, ATT_HEAD_DIM)
    v = proj[..., qd + kd:].reshape(B, T, ATT_KV_HEADS, ATT_HEAD_DIM)
    kb = _band_blocks(k, nb)
    vb = _band_blocks(v, nb)
    scores = jnp.einsum('bnicgd,bnjcd->bncgij', q, kb).astype(jnp.float32) * (ATT_HEAD_DIM ** -0.5)
    i = jnp.arange(ATT_BLOCK)
    j = jnp.arange(3 * ATT_BLOCK)
    dist = jnp.abs((j[None, :] - ATT_BLOCK) - i[:, None])
    key_pos = jnp.arange(nb)[:, None] * ATT_BLOCK - ATT_BLOCK + j[None, :]
    valid = (dist <= ATT_WINDOW)[None] & ((key_pos >= 0) & (key_pos < T))[:, None, :]
    slopes = _alibi_slopes(ATT_HEADS).reshape(ATT_KV_HEADS, ATT_GROUP)
    bias = -slopes[:, :, None, None] * dist.astype(jnp.float32)
    logits = jnp.where(valid[None, :, None, None], scores + bias, NEG_INF)
    sink_l = jnp.broadcast_to(
        sink.astype(jnp.float32).reshape(ATT_KV_HEADS, ATT_GROUP)[None, None, :, :, None, None],
        logits.shape[:-1] + (1,))
    p = jax.nn.softmax(jnp.concatenate([logits, sink_l], axis=-1), axis=-1)[..., :-1]
    out = jnp.einsum('bncgij,bnjcd->bnicgd', p.astype(vb.dtype), vb).reshape(B, T, qd)
    return out @ w_out


def _retention_causal(q, k, v, log_gamma):
    B, T, H, _ = q.shape
    n = T // RET_CHUNK

    def chunks(a):
        return a.reshape(B, n, RET_CHUNK, H, a.shape[-1]).transpose(1, 0, 3, 2, 4)

    pos = jnp.arange(RET_CHUNK, dtype=jnp.float32)
    rel = pos[:, None] - pos[None, :]
    causal = rel >= 0
    decay_in = jnp.where(causal, jnp.exp(log_gamma[:, None, None] * jnp.where(causal, rel, 0.0)), 0.0)
    xi = jnp.exp(log_gamma[:, None] * (pos + 1.0))
    zeta = jnp.exp(log_gamma[:, None] * (RET_CHUNK - 1.0 - pos))
    chunk_decay = jnp.exp(log_gamma * RET_CHUNK)[:, None, None]

    def step(state, qkv):
        qc, kc, vc = qkv
        inner = jnp.einsum('bhid,bhjd->bhij', qc, kc) * decay_in
        out = (jnp.einsum('bhij,bhje->bhie', inner, vc)
               + jnp.einsum('bhid,bhde->bhie', qc, state) * xi[..., None])
        state = chunk_decay * state + jnp.einsum('bhjd,bhje->bhde', kc * zeta[..., None], vc)
        return state, out

    state0 = jnp.zeros((B, H, q.shape[-1], v.shape[-1]), jnp.float32)
    _, out = lax.scan(step, state0, (chunks(q), chunks(k), chunks(v)))
    return out.transpose(1, 0, 3, 2, 4).reshape(B, T, H, v.shape[-1])


def _bidir_retention(x, w_in, w_out, dexp_f, dexp_b):
    B, T, _ = x.shape
    qd = RET_HEADS * RET_DK
    vd = RET_HEADS * RET_DV
    proj = x @ w_in
    q = proj[..., :qd].reshape(B, T, RET_HEADS, RET_DK).astype(jnp.float32)
    k = proj[..., qd:2 * qd].reshape(B, T, RET_HEADS, RET_DK).astype(jnp.float32) * (RET_DK ** -0.5)
    v = proj[..., 2 * qd:2 * qd + vd].reshape(B, T, RET_HEADS, RET_DV).astype(jnp.float32)
    g = proj[..., 2 * qd + vd:]
    lg_f = jnp.log1p(-jnp.exp2(-dexp_f.astype(jnp.float32)))
    lg_b = jnp.log1p(-jnp.exp2(-dexp_b.astype(jnp.float32)))
    y_f = _retention_causal(q, k, v, lg_f)
    y_b = jnp.flip(_retention_causal(jnp.flip(q, 1), jnp.flip(k, 1), jnp.flip(v, 1), lg_b), 1)
    y = y_f + y_b
    y = y * lax.rsqrt(jnp.mean(y * y, axis=-1, keepdims=True) + NORM_EPS)
    y = y.reshape(B, T, vd).astype(x.dtype) * jax.nn.silu(g)
    return y @ w_out


def _expert_choice_ffn(x, w_router, w_gate, w_up, w_down):
    B, T, D = x.shape
    n_tok = B * T
    cap = max(1, EC_CAPACITY * n_tok // N_EXPERTS)
    xt = x.reshape(n_tok, D)
    aff = jax.nn.softmax((xt @ w_router).astype(jnp.float32), axis=-1)
    gate, idx = lax.top_k(aff.T, cap)
    xe = xt[idx]
    h = jax.nn.silu(jnp.einsum('ecd,edf->ecf', xe, w_gate)) * jnp.einsum('ecd,edf->ecf', xe, w_up)
    ye = jnp.einsum('ecf,efd->ecd', h, w_down) * gate[..., None].astype(x.dtype)
    out = jnp.zeros_like(xt).at[idx.reshape(-1)].add(ye.reshape(-1, D))
    return out.reshape(B, T, D)


def _trunk(x, norm_mix, norm_ffn, norm_final, w_attn_in, w_attn_out, attn_sink,
           w_ret_in, w_ret_out, ret_decay_exp_fwd, ret_decay_exp_bwd,
           w_router, w_exp_gate, w_exp_up, w_exp_down):
    for i in range(DEPTH):
        h = _rmsnorm(x, norm_mix[i])
        j = i // N_MIXERS
        if i % N_MIXERS == 0:
            x = x + _window_attention(h, w_attn_in[j], w_attn_out[j], attn_sink[j])
        else:
            x = x + _bidir_retention(h, w_ret_in[j], w_ret_out[j], ret_decay_exp_fwd[j], ret_decay_exp_bwd[j])
        x = x + _expert_choice_ffn(_rmsnorm(x, norm_ffn[i]), w_router[i], w_exp_gate[i], w_exp_up[i], w_exp_down[i])
    return _rmsnorm(x, norm_final)


def setup_inputs(seed: int = 0) -> dict:
    key = jax.random.key(seed)
    ks = jax.random.split(key, 16)
    f32 = jnp.float32

    def nrm(k, shape, scale):
        return jax.random.normal(k, shape, dtype=f32) * scale

    base_exp = 5.0 + jnp.arange(RET_HEADS, dtype=f32)
    return {
        "x_prompt": nrm(ks[0], (BATCH, SEQ, D_MODEL), 1.0),
        "x_sample": nrm(ks[1], (DEC_BATCH, DEC_SEQ, D_MODEL), 1.0),
        "norm_mix": 1.0 + nrm(ks[2], (DEPTH, D_MODEL), 0.02),
        "norm_ffn": 1.0 + nrm(ks[3], (DEPTH, D_MODEL), 0.02),
        "norm_final": 1.0 + nrm(ks[4], (D_MODEL,), 0.02),
        "w_attn_in": nrm(ks[5], (N_ATTN_LAYERS, D_MODEL, ATT_IN), D_MODEL ** -0.5),
        "w_attn_out": nrm(ks[6], (N_ATTN_LAYERS, ATT_HEADS * ATT_HEAD_DIM, D_MODEL), (ATT_HEADS * ATT_HEAD_DIM) ** -0.5),
        "attn_sink": nrm(ks[7], (N_ATTN_LAYERS, ATT_HEADS), 0.5),
        "w_ret_in": nrm(ks[8], (N_RET_LAYERS, D_MODEL, RET_IN), D_MODEL ** -0.5),
        "w_ret_out": nrm(ks[9], (N_RET_LAYERS, RET_HEADS * RET_DV, D_MODEL), (RET_HEADS * RET_DV) ** -0.5),
        "ret_decay_exp_fwd": base_exp[None, :] + nrm(ks[10], (N_RET_LAYERS, RET_HEADS), 0.1),
        "ret_decay_exp_bwd": base_exp[None, :] + nrm(ks[11], (N_RET_LAYERS, RET_HEADS), 0.1),
        "w_router": nrm(ks[12], (DEPTH, D_MODEL, N_EXPERTS), D_MODEL ** -0.5),
        "w_exp_gate": nrm(ks[13], (DEPTH, N_EXPERTS, D_MODEL, EXPERT_FF), D_MODEL ** -0.5),
        "w_exp_up": nrm(ks[14], (DEPTH, N_EXPERTS, D_MODEL, EXPERT_FF), D_MODEL ** -0.5),
        "w_exp_down": nrm(ks[15], (DEPTH, N_EXPERTS, EXPERT_FF, D_MODEL), EXPERT_FF ** -0.5),
    }


def reference(x_prompt, x_sample, norm_mix, norm_ffn, norm_final, w_attn_in, w_attn_out, attn_sink,
              w_ret_in, w_ret_out, ret_decay_exp_fwd, ret_decay_exp_bwd,
              w_router, w_exp_gate, w_exp_up, w_exp_down):
    y_prompt = _trunk(x_prompt, norm_mix, norm_ffn, norm_final, w_attn_in, w_attn_out, attn_sink,
                      w_ret_in, w_ret_out, ret_decay_exp_fwd, ret_decay_exp_bwd,
                      w_router, w_exp_gate, w_exp_up, w_exp_down)
    y_sample = _trunk(x_sample, norm_mix, norm_ffn, norm_final, w_attn_in, w_attn_out, attn_sink,
                      w_ret_in, w_ret_out, ret_decay_exp_fwd, ret_decay_exp_bwd,
                      w_router, w_exp_gate, w_exp_up, w_exp_down)
    return (y_prompt, y_sample)
```

```python
import functools
from typing import NamedTuple

import numpy as np
import jax
import jax.numpy as jnp
from jax import lax
from jax.experimental import pallas as pl
from jax.experimental.pallas import tpu as pltpu

BF16 = jnp.bfloat16
F32 = jnp.float32
NORM_EPS = 1e-6
NEG_INF = -1e30

V7X_VMEM_BYTES = 64 * 1024 * 1024
VMEM_LIMIT_BYTES = V7X_VMEM_BYTES - 8 * 1024 * 1024


class Config(NamedTuple):
    d_model: int
    att_heads: int
    att_kv_heads: int
    att_head_dim: int
    att_block: int
    ret_heads: int
    ret_dk: int
    ret_dv: int
    ret_chunk: int
    n_experts: int
    ec_capacity: int
    seq_lens: tuple
    group_tokens: tuple
    norm_rows: int
    mm_rows: int
    mm_cols: int
    ret_heads_per_step: int


def _params(*semantics):
    return pltpu.CompilerParams(dimension_semantics=semantics, vmem_limit_bytes=VMEM_LIMIT_BYTES)


def _rmsnorm_f32(x, g):
    return x * lax.rsqrt(jnp.mean(x * x, axis=-1, keepdims=True) + NORM_EPS) * g


def _rmsnorm_kernel(x_ref, g_ref, o_ref):
    o_ref[...] = _rmsnorm_f32(x_ref[...], g_ref[...]).astype(o_ref.dtype)


def _rmsnorm(x, g, out_dtype, rows):
    n, d = x.shape
    return pl.pallas_call(
        _rmsnorm_kernel,
        grid=(n // rows,),
        in_specs=[pl.BlockSpec((rows, d), lambda i: (i, 0)),
                  pl.BlockSpec((1, d), lambda i: (0, 0))],
        out_specs=pl.BlockSpec((rows, d), lambda i: (i, 0)),
        out_shape=jax.ShapeDtypeStruct((n, d), out_dtype),
        compiler_params=_params("parallel"),
    )(x, g.reshape(1, d))


def _split_bf16(a):
    hi = a.astype(BF16)
    lo = (a - hi.astype(F32)).astype(BF16)
    return hi, lo


def _rmsnorm_router_kernel(x_ref, g_ref, wrt_ref, o_ref, aff_ref):
    h = _rmsnorm_f32(x_ref[...], g_ref[...])
    o_ref[...] = h.astype(o_ref.dtype)
    h_hi, h_lo = _split_bf16(h)
    w_hi, w_lo = _split_bf16(wrt_ref[...])
    dims = (((1,), (1,)), ((), ()))
    logits = (lax.dot_general(w_hi, h_hi, dims, preferred_element_type=F32)
              + lax.dot_general(w_lo, h_hi, dims, preferred_element_type=F32)
              + lax.dot_general(w_hi, h_lo, dims, preferred_element_type=F32))
    m = jnp.max(logits, axis=0, keepdims=True)
    e = jnp.exp(logits - m)
    aff_ref[...] = e / jnp.sum(e, axis=0, keepdims=True)


def _rmsnorm_router(x, g, w_router, rows):
    n, d = x.shape
    n_exp = w_router.shape[1]
    return pl.pallas_call(
        _rmsnorm_router_kernel,
        grid=(n // rows,),
        in_specs=[pl.BlockSpec((rows, d), lambda i: (i, 0)),
                  pl.BlockSpec((1, d), lambda i: (0, 0)),
                  pl.BlockSpec((n_exp, d), lambda i: (0, 0))],
        out_specs=[pl.BlockSpec((rows, d), lambda i: (i, 0)),
                   pl.BlockSpec((n_exp, rows), lambda i: (0, i))],
        out_shape=[jax.ShapeDtypeStruct((n, d), BF16),
                   jax.ShapeDtypeStruct((n_exp, n), F32)],
        compiler_params=_params("parallel"),
    )(x, g.reshape(1, d), w_router.T)


def _mm_kernel(*refs, dual, scaled, residual):
    refs = list(refs)
    o_ref = refs.pop()
    x_ref, w_ref = refs[0], refs[1]
    rest = refs[2:]
    x = x_ref[...]
    acc = jnp.dot(x, w_ref[...].astype(BF16), preferred_element_type=F32)
    if dual:
        up = jnp.dot(x, rest.pop(0)[...].astype(BF16), preferred_element_type=F32)
        acc = acc * jax.nn.sigmoid(acc) * up
    if scaled:
        acc = acc * rest.pop(0)[...]
    if residual:
        acc = acc + rest.pop(0)[...]
    o_ref[...] = acc.astype(o_ref.dtype)


def _matmul(x, w, *, out_dtype, bm, bn, w2=None, scale=None, res=None):
    g_x, m, k = x.shape
    g_w, _, n = w.shape
    grid = (g_x, m // bm, n // bn)
    w_spec = pl.BlockSpec((None, k, bn), lambda g, i, j: (g % g_w, 0, j))
    x_mode = pl.Buffered(1) if w.dtype == F32 else None
    in_specs = [pl.BlockSpec((None, bm, k), lambda g, i, j: (g, i, 0), pipeline_mode=x_mode), w_spec]
    args = [x, w]
    if w2 is not None:
        in_specs.append(w_spec)
        args.append(w2)
    if scale is not None:
        in_specs.append(pl.BlockSpec((None, bm, 1), lambda g, i, j: (g, i, 0)))
        args.append(scale)
    if res is not None:
        in_specs.append(pl.BlockSpec((None, bm, bn), lambda g, i, j: (g, i, j)))
        args.append(res)
    kern = functools.partial(_mm_kernel, dual=w2 is not None, scaled=scale is not None,
                             residual=res is not None)
    return pl.pallas_call(
        kern,
        grid=grid,
        in_specs=in_specs,
        out_specs=pl.BlockSpec((None, bm, bn), lambda g, i, j: (g, i, j)),
        out_shape=jax.ShapeDtypeStruct((g_x, m, n), out_dtype),
        compiler_params=_params("parallel", "parallel", "arbitrary"),
    )(*args)


def _dense(x, w, *, cfg, out_dtype, res=None, bm=None):
    bm = bm or cfg.mm_rows
    out = _matmul(x[None], w[None], out_dtype=out_dtype, bm=bm, bn=cfg.mm_cols,
                  res=None if res is None else res[None])
    return out[0]


def _attn_kernel(first_ref, last_ref, sink_ref, q_ref, kp_ref, ko_ref, kn_ref, vp_ref, vo_ref, vn_ref,
                 bias_ref, o_ref, *, kv_heads, group, hd, blk):
    n = pl.program_id(0)
    col = lax.broadcasted_iota(jnp.int32, (1, 3 * blk), 1)
    lo = jnp.where(first_ref[n] == 1, blk, 0)
    hi = jnp.where(last_ref[n] == 1, 2 * blk, 3 * blk)
    drop = (col < lo) | (col >= hi)
    edge = jnp.where(drop, NEG_INF, 0.0).astype(F32)
    scale = hd ** -0.5
    for c in range(kv_heads):
        ks = slice(c * hd, (c + 1) * hd)
        kb = jnp.concatenate([kp_ref[:, ks], ko_ref[:, ks], kn_ref[:, ks]], axis=0)
        vb = jnp.concatenate([vp_ref[:, ks], vo_ref[:, ks], vn_ref[:, ks]], axis=0)
        qs = jnp.concatenate(
            [q_ref[:, (c * group + g) * hd:(c * group + g + 1) * hd] for g in range(group)], axis=0)
        s = lax.dot_general(qs, kb, (((1,), (1,)), ((), ())), preferred_element_type=F32)
        probs = []
        for g in range(group):
            h = c * group + g
            logits = s[g * blk:(g + 1) * blk] * scale + bias_ref[h] + edge
            sink = sink_ref[h]
            m = jnp.maximum(jnp.max(logits, axis=-1, keepdims=True), sink)
            p = jnp.exp(logits - m)
            den = jnp.sum(p, axis=-1, keepdims=True) + jnp.exp(sink - m)
            probs.append((p / den).astype(BF16))
        o = jnp.dot(jnp.concatenate(probs, axis=0), vb, preferred_element_type=F32)
        for g in range(group):
            h = c * group + g
            o_ref[:, h * hd:(h + 1) * hd] = o[g * blk:(g + 1) * blk].astype(o_ref.dtype)


def _block_flags(seq_lens, blk):
    first, last = [], []
    for t in seq_lens:
        nb = t // blk
        first += [1] + [0] * (nb - 1)
        last += [0] * (nb - 1) + [1]
    return jnp.asarray(np.array(first, np.int32)), jnp.asarray(np.array(last, np.int32))


def _alibi_bias(cfg):
    blk = cfg.att_block
    i = np.arange(blk)[:, None]
    j = np.arange(3 * blk)[None, :]
    dist = np.abs((j - blk) - i)
    slopes = jnp.exp2(-8.0 * jnp.arange(1, cfg.att_heads + 1, dtype=F32) / cfg.att_heads)
    bias = -slopes[:, None, None] * jnp.asarray(dist, F32)[None]
    return jnp.where(jnp.asarray(dist <= blk)[None], bias, NEG_INF)


def _window_attention(proj, sink, cfg):
    n_tok = proj.shape[0]
    blk, hd, kvh = cfg.att_block, cfg.att_head_dim, cfg.att_kv_heads
    group = cfg.att_heads // kvh
    qd, kd = cfg.att_heads * hd, kvh * hd
    nb = n_tok // blk
    first, last = _block_flags(cfg.seq_lens, blk)
    k_col, v_col = qd // kd, qd // kd + 1

    def spec(col, shift):
        return pl.BlockSpec((blk, kd), lambda n, *_: (jnp.clip(n + shift, 0, nb - 1), col))

    kern = functools.partial(_attn_kernel, kv_heads=kvh, group=group, hd=hd, blk=blk)
    grid_spec = pltpu.PrefetchScalarGridSpec(
        num_scalar_prefetch=2,
        grid=(nb,),
        in_specs=[pl.BlockSpec(memory_space=pltpu.SMEM),
                  pl.BlockSpec((blk, qd), lambda n, *_: (n, 0)),
                  spec(k_col, -1), spec(k_col, 0), spec(k_col, 1),
                  spec(v_col, -1), spec(v_col, 0), spec(v_col, 1),
                  pl.BlockSpec((cfg.att_heads, blk, 3 * blk), lambda n, *_: (0, 0, 0))],
        out_specs=pl.BlockSpec((blk, qd), lambda n, *_: (n, 0)),
    )
    return pl.pallas_call(
        kern,
        grid_spec=grid_spec,
        out_shape=jax.ShapeDtypeStruct((n_tok, qd), BF16),
        compiler_params=_params("parallel"),
    )(first, last, sink.astype(F32), proj, proj, proj, proj, proj, proj, proj, _alibi_bias(cfg))


def _lane_tile(a, reps):
    return jnp.concatenate([a] * reps, axis=1)


def _ret_kernel(reset_ref, cd_ref, *refs, heads, dk, dv, chunk, final):
    if final:
        q_ref, k_ref, v_ref, xi_ref, zeta_ref, y1_ref, g_ref, o_ref, s_ref = refs
    else:
        q_ref, k_ref, v_ref, xi_ref, zeta_ref, dmat_ref, o_ref, s_ref = refs
    hg = pl.program_id(0)
    n = pl.program_id(1)

    @pl.when(reset_ref[n] == 1)
    def _():
        s_ref[...] = jnp.zeros_like(s_ref)

    for h in range(heads):
        q = q_ref[:, h * dk:(h + 1) * dk]
        k = k_ref[:, h * dk:(h + 1) * dk]
        v = v_ref[:, h * dv:(h + 1) * dv]
        state = s_ref[h]
        y = jnp.dot(q, state.astype(BF16), preferred_element_type=F32) * _lane_tile(xi_ref[h], dv // chunk)
        if final:
            y = y + y1_ref[:, h * dv:(h + 1) * dv]
        else:
            a = lax.dot_general(q, k, (((1,), (1,)), ((), ())), preferred_element_type=F32)
            y = y + jnp.dot((a * dmat_ref[h]).astype(BF16), v, preferred_element_type=F32)
        kz = (k.astype(F32) * _lane_tile(zeta_ref[h], dk // chunk)).astype(BF16)
        s_ref[h] = cd_ref[hg * heads + h] * state + lax.dot_general(
            kz, v, (((0,), (0,)), ((), ())), preferred_element_type=F32)
        if final:
            yn = y * lax.rsqrt(jnp.mean(y * y, axis=-1, keepdims=True) + NORM_EPS)
            gate = g_ref[:, h * dv:(h + 1) * dv].astype(F32)
            o_ref[:, h * dv:(h + 1) * dv] = (yn * (gate * jax.nn.sigmoid(gate))).astype(o_ref.dtype)
        else:
            o_ref[:, h * dv:(h + 1) * dv] = y


def _ret_tables(log_gamma, chunk, dk, backward):
    pos = jnp.arange(chunk, dtype=F32)
    lg = log_gamma[:, None]
    if backward:
        xi = jnp.exp(lg * (chunk - pos))
        zeta = jnp.exp(lg * pos)
    else:
        xi = jnp.exp(lg * (pos + 1.0))
        zeta = jnp.exp(lg * (chunk - 1.0 - pos))
    scale = dk ** -0.5
    ones = jnp.ones((1, 1, chunk), F32)
    xi_t = xi[:, :, None] * ones
    zeta_t = (zeta * scale)[:, :, None] * ones
    cd = jnp.exp(log_gamma * chunk)
    return xi_t, zeta_t, cd


def _ret_intra_decay(lg_f, lg_b, chunk, dk):
    pos = jnp.arange(chunk, dtype=F32)
    rel = pos[:, None] - pos[None, :]
    d_f = jnp.where(rel >= 0, jnp.exp(lg_f[:, None, None] * jnp.maximum(rel, 0.0)), 0.0)
    d_b = jnp.where(rel <= 0, jnp.exp(lg_b[:, None, None] * jnp.maximum(-rel, 0.0)), 0.0)
    return (d_f + d_b) * dk ** -0.5


def _retention(proj, dexp_f, dexp_b, cfg):
    n_tok = proj.shape[0]
    heads, dk, dv, chunk = cfg.ret_heads, cfg.ret_dk, cfg.ret_dv, cfg.ret_chunk
    hps = cfg.ret_heads_per_step
    n_hg = heads // hps
    nc = n_tok // chunk
    first, last = _block_flags(cfg.seq_lens, chunk)
    lg_f = jnp.log1p(-jnp.exp2(-dexp_f.astype(F32)))
    lg_b = jnp.log1p(-jnp.exp2(-dexp_b.astype(F32)))
    xi_f, zeta_f, cd_f = _ret_tables(lg_f, chunk, dk, backward=False)
    xi_b, zeta_b, cd_b = _ret_tables(lg_b, chunk, dk, backward=True)
    dmat = _ret_intra_decay(lg_f, lg_b, chunk, dk)
    k_col0 = heads * dk // (hps * dk)
    v_col0 = 2 * heads * dk // (hps * dv)
    g_col0 = v_col0 + n_hg

    def run(final, reset, cd, xi, zeta, extra_in, extra_specs, out_dtype):
        def tok(n):
            return (nc - 1 - n) if final else n

        tbl = pl.BlockSpec((hps, chunk, chunk), lambda hg, n, *_: (hg, 0, 0))
        in_specs = [pl.BlockSpec(memory_space=pltpu.SMEM),
                    pl.BlockSpec((chunk, hps * dk), lambda hg, n, *_: (tok(n), hg)),
                    pl.BlockSpec((chunk, hps * dk), lambda hg, n, *_: (tok(n), k_col0 + hg)),
                    pl.BlockSpec((chunk, hps * dv), lambda hg, n, *_: (tok(n), v_col0 + hg)),
                    tbl, tbl] + extra_specs(tok)
        kern = functools.partial(_ret_kernel, heads=hps, dk=dk, dv=dv, chunk=chunk, final=final)
        grid_spec = pltpu.PrefetchScalarGridSpec(
            num_scalar_prefetch=1,
            grid=(n_hg, nc),
            in_specs=in_specs,
            out_specs=pl.BlockSpec((chunk, hps * dv), lambda hg, n, *_: (tok(n), hg)),
            scratch_shapes=[pltpu.VMEM((hps, dk, dv), F32)],
        )
        return pl.pallas_call(
            kern,
            grid_spec=grid_spec,
            out_shape=jax.ShapeDtypeStruct((n_tok, heads * dv), out_dtype),
            compiler_params=_params("parallel", "arbitrary"),
        )(reset, cd, proj, proj, proj, xi, zeta, *extra_in)

    y1 = run(False, first, cd_f, xi_f, zeta_f, [dmat],
             lambda tok: [pl.BlockSpec((hps, chunk, chunk), lambda hg, n, *_: (hg, 0, 0))], F32)
    return run(True, last[::-1], cd_b, xi_b, zeta_b, [y1, proj],
               lambda tok: [pl.BlockSpec((chunk, hps * dv), lambda hg, n, *_: (tok(n), hg)),
                            pl.BlockSpec((chunk, hps * dv), lambda hg, n, *_: (tok(n), g_col0 + hg))],
               BF16)


def _expert_choice_ffn(x, h, aff, w_gate, w_up, w_down, cfg):
    n_exp = cfg.n_experts
    d = x.shape[1]
    idx_all, gate_all = [], []
    start = 0
    for n_tok in cfg.group_tokens:
        cap = max(1, cfg.ec_capacity * n_tok // n_exp)
        gate, idx = lax.top_k(aff[:, start:start + n_tok], cap)
        idx_all.append(idx + start)
        gate_all.append(gate)
        start += n_tok
    caps = {i.shape[1] for i in idx_all}
    assert len(caps) == 1
    cap = caps.pop()
    idx = jnp.concatenate(idx_all, axis=0)
    gate = jnp.concatenate(gate_all, axis=0)
    xe = h[idx]
    hid = _matmul(xe, w_gate, w2=w_up, out_dtype=BF16, bm=cap, bn=cfg.mm_cols)
    ye = _matmul(hid, w_down, scale=gate[..., None], out_dtype=F32, bm=cap, bn=cfg.mm_cols)
    return x.at[idx.reshape(-1)].add(ye.reshape(-1, d))


def _trunk(x, cfg, norm_mix, norm_ffn, norm_final, w_attn_in, w_attn_out, attn_sink,
           w_ret_in, w_ret_out, dexp_f, dexp_b, w_router, w_exp_gate, w_exp_up, w_exp_down):
    depth = norm_mix.shape[0]
    for i in range(depth):
        h = _rmsnorm(x, norm_mix[i], BF16, cfg.norm_rows)
        j = i // 2
        if i % 2 == 0:
            proj = _dense(h, w_attn_in[j].astype(BF16), cfg=cfg, out_dtype=BF16)
            mixed = _window_attention(proj, attn_sink[j], cfg)
            x = _dense(mixed, w_attn_out[j].astype(BF16), cfg=cfg, out_dtype=F32, res=x)
        else:
            proj = _dense(h, w_ret_in[j].astype(BF16), cfg=cfg, out_dtype=BF16)
            mixed = _retention(proj, dexp_f[j], dexp_b[j], cfg)
            x = _dense(mixed, w_ret_out[j].astype(BF16), cfg=cfg, out_dtype=F32, res=x,
                       bm=cfg.mm_rows // 2)
        h2, aff = _rmsnorm_router(x, norm_ffn[i], w_router[i], cfg.norm_rows)
        x = _expert_choice_ffn(x, h2, aff, w_exp_gate[i], w_exp_up[i], w_exp_down[i], cfg)
    return _rmsnorm(x, norm_final, F32, cfg.norm_rows)


def _config(x_prompt, x_sample, att_heads, ret_heads, n_experts):
    b, t, d = x_prompt.shape
    db, dt, _ = x_sample.shape
    return Config(
        d_model=d, att_heads=att_heads, att_kv_heads=att_heads // 4, att_head_dim=d // att_heads,
        att_block=128, ret_heads=ret_heads, ret_dk=d // ret_heads, ret_dv=2 * d // ret_heads,
        ret_chunk=128, n_experts=n_experts, ec_capacity=2,
        seq_lens=(t,) * b + (dt,) * db, group_tokens=(b * t, db * dt),
        norm_rows=256, mm_rows=2048, mm_cols=256, ret_heads_per_step=4)


def kernel(x_prompt, x_sample, norm_mix, norm_ffn, norm_final, w_attn_in, w_attn_out, attn_sink,
           w_ret_in, w_ret_out, ret_decay_exp_fwd, ret_decay_exp_bwd,
           w_router, w_exp_gate, w_exp_up, w_exp_down):
    cfg = _config(x_prompt, x_sample, attn_sink.shape[1], ret_decay_exp_fwd.shape[1], w_router.shape[2])
    d = cfg.d_model
    x = jnp.concatenate([x_prompt.reshape(-1, d), x_sample.reshape(-1, d)], axis=0)
    y = _trunk(x, cfg, norm_mix, norm_ffn, norm_final, w_attn_in, w_attn_out, attn_sink,
               w_ret_in, w_ret_out, ret_decay_exp_fwd, ret_decay_exp_bwd,
               w_router, w_exp_gate, w_exp_up, w_exp_down)
    n_prompt = cfg.group_tokens[0]
    return y[:n_prompt].reshape(x_prompt.shape), y[n_prompt:].reshape(x_sample.shape)
```

```python
import functools
from typing import NamedTuple

import numpy as np
import jax
import jax.numpy as jnp
from jax import lax
from jax.experimental import pallas as pl
from jax.experimental.pallas import tpu as pltpu

BF16 = jnp.bfloat16
F32 = jnp.float32
NORM_EPS = 1e-6
NEG_INF = -1e30

V7X_VMEM_BYTES = 64 * 1024 * 1024
VMEM_LIMIT_BYTES = V7X_VMEM_BYTES - 8 * 1024 * 1024


class Config(NamedTuple):
    d_model: int
    att_heads: int
    att_kv_heads: int
    att_head_dim: int
    att_block: int
    ret_heads: int
    ret_dk: int
    ret_dv: int
    ret_chunk: int
    n_experts: int
    ec_capacity: int
    seq_lens: tuple
    group_tokens: tuple
    norm_rows: int
    mm_rows: int
    mm_cols: int
    ret_heads_per_step: int
    combine_rows: int


def _params(*semantics, **kwargs):
    return pltpu.CompilerParams(dimension_semantics=semantics, vmem_limit_bytes=VMEM_LIMIT_BYTES, **kwargs)


def _rmsnorm_f32(x, g):
    return x * lax.rsqrt(jnp.mean(x * x, axis=-1, keepdims=True) + NORM_EPS) * g


def _rmsnorm_kernel(x_ref, g_ref, o_ref):
    o_ref[...] = _rmsnorm_f32(x_ref[...], g_ref[...]).astype(o_ref.dtype)


def _rmsnorm(x, g, out_dtype, rows):
    n, d = x.shape
    return pl.pallas_call(
        _rmsnorm_kernel,
        grid=(n // rows,),
        in_specs=[pl.BlockSpec((rows, d), lambda i: (i, 0)),
                  pl.BlockSpec((1, d), lambda i: (0, 0))],
        out_specs=pl.BlockSpec((rows, d), lambda i: (i, 0)),
        out_shape=jax.ShapeDtypeStruct((n, d), out_dtype),
        compiler_params=_params("parallel"),
    )(x, g.reshape(1, d))


def _split_bf16(a):
    hi = a.astype(BF16)
    lo = (a - hi.astype(F32)).astype(BF16)
    return hi, lo


def _rmsnorm_router_kernel(x_ref, g_ref, wrt_ref, o_ref, aff_ref):
    h = _rmsnorm_f32(x_ref[...], g_ref[...])
    o_ref[...] = h.astype(o_ref.dtype)
    h_hi, h_lo = _split_bf16(h)
    w_hi, w_lo = _split_bf16(wrt_ref[...])
    dims = (((1,), (1,)), ((), ()))
    logits = (lax.dot_general(w_hi, h_hi, dims, preferred_element_type=F32)
              + lax.dot_general(w_lo, h_hi, dims, preferred_element_type=F32)
              + lax.dot_general(w_hi, h_lo, dims, preferred_element_type=F32))
    m = jnp.max(logits, axis=0, keepdims=True)
    e = jnp.exp(logits - m)
    aff_ref[...] = e / jnp.sum(e, axis=0, keepdims=True)


def _rmsnorm_router(x, g, w_router, rows):
    n, d = x.shape
    n_exp = w_router.shape[1]
    return pl.pallas_call(
        _rmsnorm_router_kernel,
        grid=(n // rows,),
        in_specs=[pl.BlockSpec((rows, d), lambda i: (i, 0)),
                  pl.BlockSpec((1, d), lambda i: (0, 0)),
                  pl.BlockSpec((n_exp, d), lambda i: (0, 0))],
        out_specs=[pl.BlockSpec((rows, d), lambda i: (i, 0)),
                   pl.BlockSpec((n_exp, rows), lambda i: (0, i))],
        out_shape=[jax.ShapeDtypeStruct((n, d), BF16),
                   jax.ShapeDtypeStruct((n_exp, n), F32)],
        compiler_params=_params("parallel"),
    )(x, g.reshape(1, d), w_router.T)


def _mm_kernel(*refs, dual, scaled, residual):
    refs = list(refs)
    o_ref = refs.pop()
    x_ref, w_ref = refs[0], refs[1]
    rest = refs[2:]
    x = x_ref[...]
    acc = jnp.dot(x, w_ref[...].astype(BF16), preferred_element_type=F32)
    if dual:
        up = jnp.dot(x, rest.pop(0)[...].astype(BF16), preferred_element_type=F32)
        acc = acc * jax.nn.sigmoid(acc) * up
    if scaled:
        acc = acc * rest.pop(0)[...]
    if residual:
        acc = acc + rest.pop(0)[...]
    o_ref[...] = acc.astype(o_ref.dtype)


def _matmul(x, w, *, out_dtype, bm, bn, w2=None, scale=None, res=None, w_base=0, w_period=1):
    g_x, m, k = x.shape
    n = w.shape[2]
    grid = (g_x, m // bm, n // bn)
    w_spec = pl.BlockSpec((None, k, bn), lambda g, i, j: (w_base + g % w_period, 0, j))
    x_mode = pl.Buffered(1) if w.dtype == F32 else None
    in_specs = [pl.BlockSpec((None, bm, k), lambda g, i, j: (g, i, 0), pipeline_mode=x_mode), w_spec]
    args = [x, w]
    if w2 is not None:
        in_specs.append(w_spec)
        args.append(w2)
    if scale is not None:
        in_specs.append(pl.BlockSpec((None, bm, 1), lambda g, i, j: (g, i, 0)))
        args.append(scale)
    if res is not None:
        in_specs.append(pl.BlockSpec((None, bm, bn), lambda g, i, j: (g, i, j)))
        args.append(res)
    kern = functools.partial(_mm_kernel, dual=w2 is not None, scaled=scale is not None,
                             residual=res is not None)
    return pl.pallas_call(
        kern,
        grid=grid,
        in_specs=in_specs,
        out_specs=pl.BlockSpec((None, bm, bn), lambda g, i, j: (g, i, j)),
        out_shape=jax.ShapeDtypeStruct((g_x, m, n), out_dtype),
        compiler_params=_params("parallel", "parallel", "arbitrary"),
    )(*args)


def _dense(x, w, *, out_dtype, bm, bn, res=None):
    out = _matmul(x[None], w[None], out_dtype=out_dtype, bm=bm, bn=bn,
                  res=None if res is None else res[None])
    return out[0]


def _attn_kernel(first_ref, last_ref, sink_ref, q_ref, kp_ref, ko_ref, kn_ref, vp_ref, vo_ref, vn_ref,
                 bias_ref, o_ref, *, kv_heads, group, hd, blk):
    n = pl.program_id(0)
    col = lax.broadcasted_iota(jnp.int32, (1, 3 * blk), 1)
    lo = jnp.where(first_ref[n] == 1, blk, 0)
    hi = jnp.where(last_ref[n] == 1, 2 * blk, 3 * blk)
    drop = (col < lo) | (col >= hi)
    edge = jnp.where(drop, NEG_INF, 0.0).astype(F32)
    scale = hd ** -0.5
    for c in range(kv_heads):
        ks = slice(c * hd, (c + 1) * hd)
        kb = jnp.concatenate([kp_ref[:, ks], ko_ref[:, ks], kn_ref[:, ks]], axis=0)
        vb = jnp.concatenate([vp_ref[:, ks], vo_ref[:, ks], vn_ref[:, ks]], axis=0)
        qs = jnp.concatenate(
            [q_ref[:, (c * group + g) * hd:(c * group + g + 1) * hd] for g in range(group)], axis=0)
        s = lax.dot_general(qs, kb, (((1,), (1,)), ((), ())), preferred_element_type=F32)
        probs = []
        for g in range(group):
            h = c * group + g
            logits = s[g * blk:(g + 1) * blk] * scale + bias_ref[h] + edge
            sink = sink_ref[h]
            m = jnp.maximum(jnp.max(logits, axis=-1, keepdims=True), sink)
            p = jnp.exp(logits - m)
            den = jnp.sum(p, axis=-1, keepdims=True) + jnp.exp(sink - m)
            probs.append((p / den).astype(BF16))
        o = jnp.dot(jnp.concatenate(probs, axis=0), vb, preferred_element_type=F32)
        for g in range(group):
            h = c * group + g
            o_ref[:, h * hd:(h + 1) * hd] = o[g * blk:(g + 1) * blk].astype(o_ref.dtype)


def _block_flags(seq_lens, blk):
    first, last = [], []
    for t in seq_lens:
        nb = t // blk
        first += [1] + [0] * (nb - 1)
        last += [0] * (nb - 1) + [1]
    return jnp.asarray(np.array(first, np.int32)), jnp.asarray(np.array(last, np.int32))


def _alibi_bias(cfg):
    blk = cfg.att_block
    i = np.arange(blk)[:, None]
    j = np.arange(3 * blk)[None, :]
    dist = np.abs((j - blk) - i)
    slopes = jnp.exp2(-8.0 * jnp.arange(1, cfg.att_heads + 1, dtype=F32) / cfg.att_heads)
    bias = -slopes[:, None, None] * jnp.asarray(dist, F32)[None]
    return jnp.where(jnp.asarray(dist <= blk)[None], bias, NEG_INF)


def _window_attention(proj, sink, cfg):
    n_tok = proj.shape[0]
    blk, hd, kvh = cfg.att_block, cfg.att_head_dim, cfg.att_kv_heads
    group = cfg.att_heads // kvh
    qd, kd = cfg.att_heads * hd, kvh * hd
    nb = n_tok // blk
    first, last = _block_flags(cfg.seq_lens, blk)
    k_col, v_col = qd // kd, qd // kd + 1

    def spec(col, shift):
        return pl.BlockSpec((blk, kd), lambda n, *_: (jnp.clip(n + shift, 0, nb - 1), col))

    kern = functools.partial(_attn_kernel, kv_heads=kvh, group=group, hd=hd, blk=blk)
    grid_spec = pltpu.PrefetchScalarGridSpec(
        num_scalar_prefetch=2,
        grid=(nb,),
        in_specs=[pl.BlockSpec(memory_space=pltpu.SMEM),
                  pl.BlockSpec((blk, qd), lambda n, *_: (n, 0)),
                  spec(k_col, -1), spec(k_col, 0), spec(k_col, 1),
                  spec(v_col, -1), spec(v_col, 0), spec(v_col, 1),
                  pl.BlockSpec((cfg.att_heads, blk, 3 * blk), lambda n, *_: (0, 0, 0))],
        out_specs=pl.BlockSpec((blk, qd), lambda n, *_: (n, 0)),
    )
    return pl.pallas_call(
        kern,
        grid_spec=grid_spec,
        out_shape=jax.ShapeDtypeStruct((n_tok, qd), BF16),
        compiler_params=_params("parallel"),
    )(first, last, sink.astype(F32), proj, proj, proj, proj, proj, proj, proj, _alibi_bias(cfg))


def _lane_tile(a, reps):
    return jnp.concatenate([a] * reps, axis=1)


def _ret_kernel(reset_ref, cd_ref, *refs, heads, dk, dv, chunk, final):
    if final:
        q_ref, k_ref, v_ref, xi_ref, zeta_ref, y1_ref, g_ref, o_ref, s_ref = refs
    else:
        q_ref, k_ref, v_ref, xi_ref, zeta_ref, dmat_ref, o_ref, s_ref = refs
    hg = pl.program_id(0)
    n = pl.program_id(1)

    @pl.when(reset_ref[n] == 1)
    def _():
        s_ref[...] = jnp.zeros_like(s_ref)

    for h in range(heads):
        q = q_ref[:, h * dk:(h + 1) * dk]
        k = k_ref[:, h * dk:(h + 1) * dk]
        v = v_ref[:, h * dv:(h + 1) * dv]
        state = s_ref[h]
        y = jnp.dot(q, state.astype(BF16), preferred_element_type=F32) * _lane_tile(xi_ref[h], dv // chunk)
        if final:
            y = y + y1_ref[:, h * dv:(h + 1) * dv]
        else:
            a = lax.dot_general(q, k, (((1,), (1,)), ((), ())), preferred_element_type=F32)
            y = y + jnp.dot((a * dmat_ref[h]).astype(BF16), v, preferred_element_type=F32)
        kz = (k.astype(F32) * _lane_tile(zeta_ref[h], dk // chunk)).astype(BF16)
        s_ref[h] = cd_ref[hg * heads + h] * state + lax.dot_general(
            kz, v, (((0,), (0,)), ((), ())), preferred_element_type=F32)
        if final:
            yn = y * lax.rsqrt(jnp.mean(y * y, axis=-1, keepdims=True) + NORM_EPS)
            gate = g_ref[:, h * dv:(h + 1) * dv].astype(F32)
            o_ref[:, h * dv:(h + 1) * dv] = (yn * (gate * jax.nn.sigmoid(gate))).astype(o_ref.dtype)
        else:
            o_ref[:, h * dv:(h + 1) * dv] = y


def _ret_tables(log_gamma, chunk, dk, backward):
    pos = jnp.arange(chunk, dtype=F32)
    lg = log_gamma[:, None]
    if backward:
        xi = jnp.exp(lg * (chunk - pos))
        zeta = jnp.exp(lg * pos)
    else:
        xi = jnp.exp(lg * (pos + 1.0))
        zeta = jnp.exp(lg * (chunk - 1.0 - pos))
    scale = dk ** -0.5
    ones = jnp.ones((1, 1, chunk), F32)
    xi_t = xi[:, :, None] * ones
    zeta_t = (zeta * scale)[:, :, None] * ones
    cd = jnp.exp(log_gamma * chunk)
    return xi_t, zeta_t, cd


def _ret_intra_decay(lg_f, lg_b, chunk, dk):
    pos = jnp.arange(chunk, dtype=F32)
    rel = pos[:, None] - pos[None, :]
    d_f = jnp.where(rel >= 0, jnp.exp(lg_f[:, None, None] * jnp.maximum(rel, 0.0)), 0.0)
    d_b = jnp.where(rel <= 0, jnp.exp(lg_b[:, None, None] * jnp.maximum(-rel, 0.0)), 0.0)
    return (d_f + d_b) * dk ** -0.5


def _retention(proj, dexp_f, dexp_b, cfg):
    n_tok = proj.shape[0]
    heads, dk, dv, chunk = cfg.ret_heads, cfg.ret_dk, cfg.ret_dv, cfg.ret_chunk
    hps = cfg.ret_heads_per_step
    n_hg = heads // hps
    nc = n_tok // chunk
    first, last = _block_flags(cfg.seq_lens, chunk)
    lg_f = jnp.log1p(-jnp.exp2(-dexp_f.astype(F32)))
    lg_b = jnp.log1p(-jnp.exp2(-dexp_b.astype(F32)))
    xi_f, zeta_f, cd_f = _ret_tables(lg_f, chunk, dk, backward=False)
    xi_b, zeta_b, cd_b = _ret_tables(lg_b, chunk, dk, backward=True)
    dmat = _ret_intra_decay(lg_f, lg_b, chunk, dk)
    k_col0 = heads * dk // (hps * dk)
    v_col0 = 2 * heads * dk // (hps * dv)
    g_col0 = v_col0 + n_hg

    def run(final, reset, cd, xi, zeta, extra_in, extra_specs, out_dtype):
        def tok(n):
            return (nc - 1 - n) if final else n

        tbl = pl.BlockSpec((hps, chunk, chunk), lambda hg, n, *_: (hg, 0, 0))
        in_specs = [pl.BlockSpec(memory_space=pltpu.SMEM),
                    pl.BlockSpec((chunk, hps * dk), lambda hg, n, *_: (tok(n), hg)),
                    pl.BlockSpec((chunk, hps * dk), lambda hg, n, *_: (tok(n), k_col0 + hg)),
                    pl.BlockSpec((chunk, hps * dv), lambda hg, n, *_: (tok(n), v_col0 + hg)),
                    tbl, tbl] + extra_specs(tok)
        kern = functools.partial(_ret_kernel, heads=hps, dk=dk, dv=dv, chunk=chunk, final=final)
        grid_spec = pltpu.PrefetchScalarGridSpec(
            num_scalar_prefetch=1,
            grid=(n_hg, nc),
            in_specs=in_specs,
            out_specs=pl.BlockSpec((chunk, hps * dv), lambda hg, n, *_: (tok(n), hg)),
            scratch_shapes=[pltpu.VMEM((hps, dk, dv), F32)],
        )
        return pl.pallas_call(
            kern,
            grid_spec=grid_spec,
            out_shape=jax.ShapeDtypeStruct((n_tok, heads * dv), out_dtype),
            compiler_params=_params("parallel", "arbitrary"),
        )(reset, cd, proj, proj, proj, xi, zeta, *extra_in)

    y1 = run(False, first, cd_f, xi_f, zeta_f, [dmat],
             lambda tok: [pl.BlockSpec((hps, chunk, chunk), lambda hg, n, *_: (hg, 0, 0))], F32)
    return run(True, last[::-1], cd_b, xi_b, zeta_b, [y1, proj],
               lambda tok: [pl.BlockSpec((chunk, hps * dv), lambda hg, n, *_: (tok(n), hg)),
                            pl.BlockSpec((chunk, hps * dv), lambda hg, n, *_: (tok(n), g_col0 + hg))],
               BF16)


def _combine_kernel(n_rows_ref, k_max_ref, src_ref, dst_ref, src_next_ref, dst_next_ref,
                    x_ref, tok_cnt_ref, ye_ref, o_ref, buf_ref, sem_ref, *, tb):
    r = pl.program_id(0)
    slot = r % 2

    def start_rows(src, dst, n, to_slot):
        def body(i, carry):
            pltpu.make_async_copy(ye_ref.at[src[i]], buf_ref.at[to_slot, dst[i]], sem_ref.at[to_slot]).start()
            return carry
        lax.fori_loop(0, n, body, 0)

    @pl.when(r == 0)
    def _():
        start_rows(src_ref, dst_ref, n_rows_ref[0], 0)

    @pl.when(r + 1 < pl.num_programs(0))
    def _():
        start_rows(src_next_ref, dst_next_ref, n_rows_ref[r + 1], 1 - slot)

    def wait_row(i, carry):
        pltpu.make_async_copy(ye_ref.at[0], buf_ref.at[slot, 0], sem_ref.at[slot]).wait()
        return carry
    lax.fori_loop(0, n_rows_ref[r], wait_row, 0)

    tok_cnt = tok_cnt_ref[...]
    o_ref[...] = x_ref[...]

    def add_level(k, carry):
        rows = buf_ref[slot, pl.ds(pl.multiple_of(k * tb, tb), tb), :]
        o_ref[...] += jnp.where(k < tok_cnt, rows, 0.0)
        return carry
    lax.fori_loop(0, k_max_ref[r], add_level, 0)


def _combine(x, ye, idx, cfg):
    n_tok, d = x.shape
    n_rows = idx.shape[0]
    tb = cfg.combine_rows
    max_k = cfg.n_experts
    cap = tb * max_k
    nblk = n_tok // tb
    order = jnp.argsort(idx, stable=True).astype(jnp.int32)
    tok_sorted = idx[order]
    tok_start = jnp.searchsorted(tok_sorted, jnp.arange(n_tok + 1, dtype=jnp.int32), side="left").astype(jnp.int32)
    tok_cnt = tok_start[1:] - tok_start[:-1]
    rank = jnp.arange(n_rows, dtype=jnp.int32) - tok_start[tok_sorted]
    dst_sorted = rank * tb + tok_sorted % tb
    blk_start = tok_start[:-1:tb]
    blk_rows = tok_start[tb::tb] - blk_start
    take = jnp.minimum(blk_start[:, None] + jnp.arange(cap, dtype=jnp.int32)[None, :], n_rows - 1)
    src_tbl = order[take].reshape(-1)
    dst_tbl = dst_sorted[take].reshape(-1)
    k_max = jnp.max(tok_cnt.reshape(nblk, tb), axis=1)

    def tbl_spec(shift):
        return pl.BlockSpec((cap,), lambda r, *_: (jnp.minimum(r + shift, nblk - 1),),
                            memory_space=pltpu.SMEM)

    grid_spec = pltpu.PrefetchScalarGridSpec(
        num_scalar_prefetch=2,
        grid=(nblk,),
        in_specs=[tbl_spec(0), tbl_spec(0), tbl_spec(1), tbl_spec(1),
                  pl.BlockSpec((tb, d), lambda r, *_: (r, 0)),
                  pl.BlockSpec((tb, 1), lambda r, *_: (r, 0)),
                  pl.BlockSpec(memory_space=pl.ANY)],
        out_specs=pl.BlockSpec((tb, d), lambda r, *_: (r, 0)),
        scratch_shapes=[pltpu.VMEM((2, cap, d), F32), pltpu.SemaphoreType.DMA((2,))],
    )
    return pl.pallas_call(
        functools.partial(_combine_kernel, tb=tb),
        grid_spec=grid_spec,
        out_shape=jax.ShapeDtypeStruct((n_tok, d), F32),
        compiler_params=_params("arbitrary", disable_bounds_checks=True),
    )(blk_rows, k_max, src_tbl, dst_tbl, src_tbl, dst_tbl, x, tok_cnt[:, None], ye)


def _expert_choice_ffn(x, h, aff, w_gate, w_up, w_down, layer, cfg):
    n_exp = cfg.n_experts
    d = x.shape[1]
    idx_all, gate_all = [], []
    start = 0
    for n_tok in cfg.group_tokens:
        cap = max(1, cfg.ec_capacity * n_tok // n_exp)
        gate, idx = lax.top_k(aff[:, start:start + n_tok], cap)
        idx_all.append(idx + start)
        gate_all.append(gate)
        start += n_tok
    caps = {i.shape[1] for i in idx_all}
    assert len(caps) == 1
    cap = caps.pop()
    idx = jnp.concatenate(idx_all, axis=0)
    gate = jnp.concatenate(gate_all, axis=0)
    xe = h[idx]
    experts = dict(w_base=layer * n_exp, w_period=n_exp, bm=cap, bn=cfg.mm_cols)
    hid = _matmul(xe, w_gate, w2=w_up, out_dtype=BF16, **experts)
    ye = _matmul(hid, w_down, scale=gate[..., None], out_dtype=F32, **experts)
    return _combine(x, ye.reshape(-1, d), idx.reshape(-1), cfg)


def _trunk(x, cfg, norm_mix, norm_ffn, norm_final, w_attn_in, w_attn_out, attn_sink,
           w_ret_in, w_ret_out, dexp_f, dexp_b, w_router, w_exp_gate, w_exp_up, w_exp_down):
    depth = norm_mix.shape[0]
    w_exp_gate, w_exp_up, w_exp_down = (w.reshape((-1,) + w.shape[2:]) for w in (w_exp_gate, w_exp_up, w_exp_down))
    rows, cols = cfg.mm_rows, cfg.mm_cols
    for i in range(depth):
        h = _rmsnorm(x, norm_mix[i], BF16, cfg.norm_rows)
        j = i // 2
        if i % 2 == 0:
            proj = _dense(h, w_attn_in[j].astype(BF16), out_dtype=BF16, bm=rows, bn=2 * cols)
            mixed = _window_attention(proj, attn_sink[j], cfg)
            x = _dense(mixed, w_attn_out[j].astype(BF16), out_dtype=F32, res=x, bm=rows // 2, bn=2 * cols)
        else:
            proj = _dense(h, w_ret_in[j].astype(BF16), out_dtype=BF16, bm=rows, bn=2 * cols)
            mixed = _retention(proj, dexp_f[j], dexp_b[j], cfg)
            x = _dense(mixed, w_ret_out[j].astype(BF16), out_dtype=F32, res=x, bm=rows // 2, bn=cols)
        h2, aff = _rmsnorm_router(x, norm_ffn[i], w_router[i], cfg.norm_rows)
        x = _expert_choice_ffn(x, h2, aff, w_exp_gate, w_exp_up, w_exp_down, i, cfg)
    return _rmsnorm(x, norm_final, F32, cfg.norm_rows)


def _config(x_prompt, x_sample, att_heads, ret_heads, n_experts):
    b, t, d = x_prompt.shape
    db, dt, _ = x_sample.shape
    return Config(
        d_model=d, att_heads=att_heads, att_kv_heads=att_heads // 4, att_head_dim=d // att_heads,
        att_block=128, ret_heads=ret_heads, ret_dk=d // ret_heads, ret_dv=2 * d // ret_heads,
        ret_chunk=128, n_experts=n_experts, ec_capacity=2,
        seq_lens=(t,) * b + (dt,) * db, group_tokens=(b * t, db * dt),
        norm_rows=256, mm_rows=2048, mm_cols=256, ret_heads_per_step=4, combine_rows=64)


def kernel(x_prompt, x_sample, norm_mix, norm_ffn, norm_final, w_attn_in, w_attn_out, attn_sink,
           w_ret_in, w_ret_out, ret_decay_exp_fwd, ret_decay_exp_bwd,
           w_router, w_exp_gate, w_exp_up, w_exp_down):
    cfg = _config(x_prompt, x_sample, attn_sink.shape[1], ret_decay_exp_fwd.shape[1], w_router.shape[2])
    d = cfg.d_model
    x = jnp.concatenate([x_prompt.reshape(-1, d), x_sample.reshape(-1, d)], axis=0)
    y = _trunk(x, cfg, norm_mix, norm_ffn, norm_final, w_attn_in, w_attn_out, attn_sink,
               w_ret_in, w_ret_out, ret_decay_exp_fwd, ret_decay_exp_bwd,
               w_router, w_exp_gate, w_exp_up, w_exp_down)
    n_prompt = cfg.group_tokens[0]
    return y[:n_prompt].reshape(x_prompt.shape), y[n_prompt:].reshape(x_sample.shape)
```

```python
import functools
from typing import NamedTuple

import numpy as np
import jax
import jax.numpy as jnp
from jax import lax
from jax.experimental import pallas as pl
from jax.experimental.pallas import tpu as pltpu

BF16 = jnp.bfloat16
F32 = jnp.float32
NORM_EPS = 1e-6
NEG_INF = -1e30

V7X_VMEM_BYTES = 64 * 1024 * 1024
VMEM_LIMIT_BYTES = V7X_VMEM_BYTES - 8 * 1024 * 1024
COMBINE_COLS = 512
LANES = 128


class Config(NamedTuple):
    d_model: int
    att_heads: int
    att_kv_heads: int
    att_head_dim: int
    att_block: int
    ret_heads: int
    ret_dk: int
    ret_dv: int
    ret_chunk: int
    n_experts: int
    ec_capacity: int
    seq_lens: tuple
    group_tokens: tuple
    norm_rows: int
    mm_rows: int
    mm_cols: int
    ret_heads_per_step: int
    combine_rows: int


def _params(*semantics, **kwargs):
    return pltpu.CompilerParams(dimension_semantics=semantics, vmem_limit_bytes=VMEM_LIMIT_BYTES, **kwargs)


def _rmsnorm_f32(x, g):
    return x * lax.rsqrt(jnp.mean(x * x, axis=-1, keepdims=True) + NORM_EPS) * g


def _rmsnorm_kernel(x_ref, g_ref, o_ref):
    o_ref[...] = _rmsnorm_f32(x_ref[...], g_ref[...]).astype(o_ref.dtype)


def _rmsnorm(x, g, out_dtype, rows):
    n, d = x.shape
    return pl.pallas_call(
        _rmsnorm_kernel,
        grid=(n // rows,),
        in_specs=[pl.BlockSpec((rows, d), lambda i: (i, 0)),
                  pl.BlockSpec((1, d), lambda i: (0, 0))],
        out_specs=pl.BlockSpec((rows, d), lambda i: (i, 0)),
        out_shape=jax.ShapeDtypeStruct((n, d), out_dtype),
        compiler_params=_params("parallel"),
    )(x, g.reshape(1, d))


def _split_bf16(a):
    hi = a.astype(BF16)
    lo = (a - hi.astype(F32)).astype(BF16)
    return hi, lo


def _rmsnorm_router_kernel(x_ref, g_ref, wrt_ref, o_ref, aff_ref):
    h = _rmsnorm_f32(x_ref[...], g_ref[...])
    o_ref[...] = h.astype(o_ref.dtype)
    h_hi, h_lo = _split_bf16(h)
    w_hi, w_lo = _split_bf16(wrt_ref[...])
    dims = (((1,), (1,)), ((), ()))
    logits = (lax.dot_general(w_hi, h_hi, dims, preferred_element_type=F32)
              + lax.dot_general(w_lo, h_hi, dims, preferred_element_type=F32)
              + lax.dot_general(w_hi, h_lo, dims, preferred_element_type=F32))
    m = jnp.max(logits, axis=0, keepdims=True)
    e = jnp.exp(logits - m)
    aff_ref[...] = e / jnp.sum(e, axis=0, keepdims=True)


def _rmsnorm_router(x, g, w_router, rows):
    n, d = x.shape
    n_exp = w_router.shape[1]
    return pl.pallas_call(
        _rmsnorm_router_kernel,
        grid=(n // rows,),
        in_specs=[pl.BlockSpec((rows, d), lambda i: (i, 0)),
                  pl.BlockSpec((1, d), lambda i: (0, 0)),
                  pl.BlockSpec((n_exp, d), lambda i: (0, 0))],
        out_specs=[pl.BlockSpec((rows, d), lambda i: (i, 0)),
                   pl.BlockSpec((n_exp, rows), lambda i: (0, i))],
        out_shape=[jax.ShapeDtypeStruct((n, d), BF16),
                   jax.ShapeDtypeStruct((n_exp, n), F32)],
        compiler_params=_params("parallel"),
    )(x, g.reshape(1, d), w_router.T)


def _mm_kernel(*refs, dual, scaled, residual):
    refs = list(refs)
    o_ref = refs.pop()
    x_ref, w_ref = refs[0], refs[1]
    rest = refs[2:]
    x = x_ref[...]
    acc = jnp.dot(x, w_ref[...].astype(BF16), preferred_element_type=F32)
    if dual:
        up = jnp.dot(x, rest.pop(0)[...].astype(BF16), preferred_element_type=F32)
        acc = acc * jax.nn.sigmoid(acc) * up
    if scaled:
        acc = acc * rest.pop(0)[...]
    if residual:
        acc = acc + rest.pop(0)[...]
    o_ref[...] = acc.astype(o_ref.dtype)


def _matmul(x, w, *, out_dtype, bm, bn, w2=None, scale=None, res=None, w_base=0, w_period=1):
    g_x, m, k = x.shape
    n = w.shape[2]
    grid = (g_x, m // bm, n // bn)
    w_spec = pl.BlockSpec((None, k, bn), lambda g, i, j: (w_base + g % w_period, 0, j))
    x_mode = pl.Buffered(1) if w.dtype == F32 else None
    in_specs = [pl.BlockSpec((None, bm, k), lambda g, i, j: (g, i, 0), pipeline_mode=x_mode), w_spec]
    args = [x, w]
    if w2 is not None:
        in_specs.append(w_spec)
        args.append(w2)
    if scale is not None:
        in_specs.append(pl.BlockSpec((None, bm, 1), lambda g, i, j: (g, i, 0)))
        args.append(scale)
    if res is not None:
        in_specs.append(pl.BlockSpec((None, bm, bn), lambda g, i, j: (g, i, j)))
        args.append(res)
    kern = functools.partial(_mm_kernel, dual=w2 is not None, scaled=scale is not None,
                             residual=res is not None)
    return pl.pallas_call(
        kern,
        grid=grid,
        in_specs=in_specs,
        out_specs=pl.BlockSpec((None, bm, bn), lambda g, i, j: (g, i, j)),
        out_shape=jax.ShapeDtypeStruct((g_x, m, n), out_dtype),
        compiler_params=_params("parallel", "parallel", "arbitrary"),
    )(*args)


def _dense(x, w, *, out_dtype, bm, bn, res=None):
    out = _matmul(x[None], w[None], out_dtype=out_dtype, bm=bm, bn=bn,
                  res=None if res is None else res[None])
    return out[0]


def _attn_kernel(first_ref, last_ref, sink_ref, q_ref, kp_ref, ko_ref, kn_ref, vp_ref, vo_ref, vn_ref,
                 bias_ref, o_ref, *, kv_heads, group, hd, blk):
    n = pl.program_id(0)
    col = lax.broadcasted_iota(jnp.int32, (1, 3 * blk), 1)
    lo = jnp.where(first_ref[n] == 1, blk, 0)
    hi = jnp.where(last_ref[n] == 1, 2 * blk, 3 * blk)
    drop = (col < lo) | (col >= hi)
    edge = jnp.where(drop, NEG_INF, 0.0).astype(F32)
    scale = hd ** -0.5
    for c in range(kv_heads):
        ks = slice(c * hd, (c + 1) * hd)
        kb = jnp.concatenate([kp_ref[:, ks], ko_ref[:, ks], kn_ref[:, ks]], axis=0)
        vb = jnp.concatenate([vp_ref[:, ks], vo_ref[:, ks], vn_ref[:, ks]], axis=0)
        qs = jnp.concatenate(
            [q_ref[:, (c * group + g) * hd:(c * group + g + 1) * hd] for g in range(group)], axis=0)
        s = lax.dot_general(qs, kb, (((1,), (1,)), ((), ())), preferred_element_type=F32)
        probs = []
        for g in range(group):
            h = c * group + g
            logits = s[g * blk:(g + 1) * blk] * scale + bias_ref[h] + edge
            sink = sink_ref[h]
            m = jnp.maximum(jnp.max(logits, axis=-1, keepdims=True), sink)
            p = jnp.exp(logits - m)
            den = jnp.sum(p, axis=-1, keepdims=True) + jnp.exp(sink - m)
            probs.append((p / den).astype(BF16))
        o = jnp.dot(jnp.concatenate(probs, axis=0), vb, preferred_element_type=F32)
        for g in range(group):
            h = c * group + g
            o_ref[:, h * hd:(h + 1) * hd] = o[g * blk:(g + 1) * blk].astype(o_ref.dtype)


def _block_flags(seq_lens, blk):
    first, last = [], []
    for t in seq_lens:
        nb = t // blk
        first += [1] + [0] * (nb - 1)
        last += [0] * (nb - 1) + [1]
    return jnp.asarray(np.array(first, np.int32)), jnp.asarray(np.array(last, np.int32))


def _alibi_bias(cfg):
    blk = cfg.att_block
    i = np.arange(blk)[:, None]
    j = np.arange(3 * blk)[None, :]
    dist = np.abs((j - blk) - i)
    slopes = jnp.exp2(-8.0 * jnp.arange(1, cfg.att_heads + 1, dtype=F32) / cfg.att_heads)
    bias = -slopes[:, None, None] * jnp.asarray(dist, F32)[None]
    return jnp.where(jnp.asarray(dist <= blk)[None], bias, NEG_INF)


def _window_attention(proj, sink, cfg):
    n_tok = proj.shape[0]
    blk, hd, kvh = cfg.att_block, cfg.att_head_dim, cfg.att_kv_heads
    group = cfg.att_heads // kvh
    qd, kd = cfg.att_heads * hd, kvh * hd
    nb = n_tok // blk
    first, last = _block_flags(cfg.seq_lens, blk)
    k_col, v_col = qd // kd, qd // kd + 1

    def spec(col, shift):
        return pl.BlockSpec((blk, kd), lambda n, *_: (jnp.clip(n + shift, 0, nb - 1), col))

    kern = functools.partial(_attn_kernel, kv_heads=kvh, group=group, hd=hd, blk=blk)
    grid_spec = pltpu.PrefetchScalarGridSpec(
        num_scalar_prefetch=2,
        grid=(nb,),
        in_specs=[pl.BlockSpec(memory_space=pltpu.SMEM),
                  pl.BlockSpec((blk, qd), lambda n, *_: (n, 0)),
                  spec(k_col, -1), spec(k_col, 0), spec(k_col, 1),
                  spec(v_col, -1), spec(v_col, 0), spec(v_col, 1),
                  pl.BlockSpec((cfg.att_heads, blk, 3 * blk), lambda n, *_: (0, 0, 0))],
        out_specs=pl.BlockSpec((blk, qd), lambda n, *_: (n, 0)),
    )
    return pl.pallas_call(
        kern,
        grid_spec=grid_spec,
        out_shape=jax.ShapeDtypeStruct((n_tok, qd), BF16),
        compiler_params=_params("parallel"),
    )(first, last, sink.astype(F32), proj, proj, proj, proj, proj, proj, proj, _alibi_bias(cfg))


def _lane_tile(a, reps):
    return jnp.concatenate([a] * reps, axis=1)


def _ret_kernel(reset_ref, cd_ref, *refs, heads, dk, dv, chunk, final):
    if final:
        q_ref, k_ref, v_ref, xi_ref, zeta_ref, y1_ref, g_ref, o_ref, s_ref = refs
    else:
        q_ref, k_ref, v_ref, xi_ref, zeta_ref, dmat_ref, o_ref, s_ref = refs
    hg = pl.program_id(0)
    n = pl.program_id(1)

    @pl.when(reset_ref[n] == 1)
    def _():
        s_ref[...] = jnp.zeros_like(s_ref)

    for h in range(heads):
        q = q_ref[:, h * dk:(h + 1) * dk]
        k = k_ref[:, h * dk:(h + 1) * dk]
        v = v_ref[:, h * dv:(h + 1) * dv]
        state = s_ref[h]
        y = jnp.dot(q, state.astype(BF16), preferred_element_type=F32) * _lane_tile(xi_ref[h], dv // chunk)
        if final:
            y = y + y1_ref[:, h * dv:(h + 1) * dv]
        else:
            a = lax.dot_general(q, k, (((1,), (1,)), ((), ())), preferred_element_type=F32)
            y = y + jnp.dot((a * dmat_ref[h]).astype(BF16), v, preferred_element_type=F32)
        kz = (k.astype(F32) * _lane_tile(zeta_ref[h], dk // chunk)).astype(BF16)
        s_ref[h] = cd_ref[hg * heads + h] * state + lax.dot_general(
            kz, v, (((0,), (0,)), ((), ())), preferred_element_type=F32)
        if final:
            yn = y * lax.rsqrt(jnp.mean(y * y, axis=-1, keepdims=True) + NORM_EPS)
            gate = g_ref[:, h * dv:(h + 1) * dv].astype(F32)
            o_ref[:, h * dv:(h + 1) * dv] = (yn * (gate * jax.nn.sigmoid(gate))).astype(o_ref.dtype)
        else:
            o_ref[:, h * dv:(h + 1) * dv] = y


def _ret_tables(log_gamma, chunk, dk, backward):
    pos = jnp.arange(chunk, dtype=F32)
    lg = log_gamma[:, None]
    if backward:
        xi = jnp.exp(lg * (chunk - pos))
        zeta = jnp.exp(lg * pos)
    else:
        xi = jnp.exp(lg * (pos + 1.0))
        zeta = jnp.exp(lg * (chunk - 1.0 - pos))
    scale = dk ** -0.5
    ones = jnp.ones((1, 1, chunk), F32)
    xi_t = xi[:, :, None] * ones
    zeta_t = (zeta * scale)[:, :, None] * ones
    cd = jnp.exp(log_gamma * chunk)
    return xi_t, zeta_t, cd


def _ret_intra_decay(lg_f, lg_b, chunk, dk):
    pos = jnp.arange(chunk, dtype=F32)
    rel = pos[:, None] - pos[None, :]
    d_f = jnp.where(rel >= 0, jnp.exp(lg_f[:, None, None] * jnp.maximum(rel, 0.0)), 0.0)
    d_b = jnp.where(rel <= 0, jnp.exp(lg_b[:, None, None] * jnp.maximum(-rel, 0.0)), 0.0)
    return (d_f + d_b) * dk ** -0.5


def _retention(proj, dexp_f, dexp_b, cfg):
    n_tok = proj.shape[0]
    heads, dk, dv, chunk = cfg.ret_heads, cfg.ret_dk, cfg.ret_dv, cfg.ret_chunk
    hps = cfg.ret_heads_per_step
    n_hg = heads // hps
    nc = n_tok // chunk
    first, last = _block_flags(cfg.seq_lens, chunk)
    lg_f = jnp.log1p(-jnp.exp2(-dexp_f.astype(F32)))
    lg_b = jnp.log1p(-jnp.exp2(-dexp_b.astype(F32)))
    xi_f, zeta_f, cd_f = _ret_tables(lg_f, chunk, dk, backward=False)
    xi_b, zeta_b, cd_b = _ret_tables(lg_b, chunk, dk, backward=True)
    dmat = _ret_intra_decay(lg_f, lg_b, chunk, dk)
    k_col0 = heads * dk // (hps * dk)
    v_col0 = 2 * heads * dk // (hps * dv)
    g_col0 = v_col0 + n_hg

    def run(final, reset, cd, xi, zeta, extra_in, extra_specs, out_dtype):
        def tok(n):
            return (nc - 1 - n) if final else n

        tbl = pl.BlockSpec((hps, chunk, chunk), lambda hg, n, *_: (hg, 0, 0))
        in_specs = [pl.BlockSpec(memory_space=pltpu.SMEM),
                    pl.BlockSpec((chunk, hps * dk), lambda hg, n, *_: (tok(n), hg)),
                    pl.BlockSpec((chunk, hps * dk), lambda hg, n, *_: (tok(n), k_col0 + hg)),
                    pl.BlockSpec((chunk, hps * dv), lambda hg, n, *_: (tok(n), v_col0 + hg)),
                    tbl, tbl] + extra_specs(tok)
        kern = functools.partial(_ret_kernel, heads=hps, dk=dk, dv=dv, chunk=chunk, final=final)
        grid_spec = pltpu.PrefetchScalarGridSpec(
            num_scalar_prefetch=1,
            grid=(n_hg, nc),
            in_specs=in_specs,
            out_specs=pl.BlockSpec((chunk, hps * dv), lambda hg, n, *_: (tok(n), hg)),
            scratch_shapes=[pltpu.VMEM((hps, dk, dv), F32)],
        )
        return pl.pallas_call(
            kern,
            grid_spec=grid_spec,
            out_shape=jax.ShapeDtypeStruct((n_tok, heads * dv), out_dtype),
            compiler_params=_params("parallel", "arbitrary"),
        )(reset, cd, proj, proj, proj, xi, zeta, *extra_in)

    y1 = run(False, first, cd_f, xi_f, zeta_f, [dmat],
             lambda tok: [pl.BlockSpec((hps, chunk, chunk), lambda hg, n, *_: (hg, 0, 0))], F32)
    return run(True, last[::-1], cd_b, xi_b, zeta_b, [y1, proj],
               lambda tok: [pl.BlockSpec((chunk, hps * dv), lambda hg, n, *_: (tok(n), hg)),
                            pl.BlockSpec((chunk, hps * dv), lambda hg, n, *_: (tok(n), g_col0 + hg))],
               BF16)


def _combine_kernel(blk_start_ref, n_rows_ref, k_max_ref, ent_ref, x_ref, tok_cnt_ref, ye_ref, o_ref,
                    buf_ref, sem_ref, *, tb, cap):
    r = pl.program_id(0)
    slot = r % 2

    cap_bits = cap.bit_length() - 1

    def start_rows(blk, to_slot):
        base = blk_start_ref[blk]

        def body(i, ent):
            following = ent_ref[base + i + 1]
            pltpu.make_async_copy(ye_ref.at[lax.shift_right_logical(ent, cap_bits)],
                                  buf_ref.at[to_slot, ent & (cap - 1)], sem_ref.at[to_slot]).start()
            return following
        lax.fori_loop(0, n_rows_ref[blk], body, ent_ref[base])

    @pl.when(r == 0)
    def _():
        start_rows(0, 0)

    @pl.when(r + 1 < pl.num_programs(0))
    def _():
        start_rows(r + 1, 1 - slot)

    def wait_row(i, carry):
        pltpu.make_async_copy(ye_ref.at[0], buf_ref.at[slot, 0], sem_ref.at[slot]).wait()
        return carry
    lax.fori_loop(0, n_rows_ref[r], wait_row, 0)

    tok_cnt = tok_cnt_ref[...]
    k_max = k_max_ref[r]
    slabs = [slice(j * LANES, (j + 1) * LANES) for j in range(x_ref.shape[1] // LANES)]
    per_chunk = COMBINE_COLS // LANES
    for c in range(0, len(slabs), per_chunk):
        chunk = slabs[c:c + per_chunk]

        def add_level(k, accs, chunk=chunk):
            live = k < tok_cnt
            level = pl.ds(pl.multiple_of(k * tb, tb), tb)
            return tuple(a + jnp.where(live, buf_ref[slot, level, s], 0.0) for a, s in zip(accs, chunk))
        sums = lax.fori_loop(0, k_max, add_level, tuple(x_ref[:, s] for s in chunk))
        for s, v in zip(chunk, sums):
            o_ref[:, s] = v


def _token_histogram(tok, n_tok):
    lo_size = 256
    hi_size = pl.cdiv(n_tok, lo_size)
    hi = (tok[:, None] // lo_size == jnp.arange(hi_size, dtype=jnp.int32)[None, :]).astype(BF16)
    lo = (tok[:, None] % lo_size == jnp.arange(lo_size, dtype=jnp.int32)[None, :]).astype(BF16)
    hist = jnp.einsum("rh,rl->hl", hi, lo, preferred_element_type=F32)
    return hist.reshape(-1)[:n_tok].astype(jnp.int32)


def _combine(x, ye, idx, cfg):
    n_tok, d = x.shape
    n_rows = idx.shape[0]
    tb = cfg.combine_rows
    max_k = cfg.n_experts
    cap = tb * max_k
    nblk = n_tok // tb
    pos = jnp.arange(n_rows, dtype=jnp.int32)
    tok_sorted, order = lax.sort((idx, pos), num_keys=1)
    new_tok = jnp.concatenate([jnp.ones((1,), bool), tok_sorted[1:] != tok_sorted[:-1]])
    rank = pos - lax.cummax(jnp.where(new_tok, pos, 0), axis=0)
    entries = order * cap + rank * tb + tok_sorted % tb
    entries = jnp.concatenate([entries, jnp.zeros((1,), jnp.int32)])
    tok_cnt = _token_histogram(idx, n_tok)
    blk_cnt = tok_cnt.reshape(nblk, tb)
    blk_rows = jnp.sum(blk_cnt, axis=1)
    blk_start = jnp.cumsum(blk_rows) - blk_rows
    k_max = jnp.max(blk_cnt, axis=1)

    grid_spec = pltpu.PrefetchScalarGridSpec(
        num_scalar_prefetch=4,
        grid=(nblk,),
        in_specs=[pl.BlockSpec((tb, d), lambda r, *_: (r, 0)),
                  pl.BlockSpec((tb, LANES), lambda r, *_: (r, 0)),
                  pl.BlockSpec(memory_space=pl.ANY)],
        out_specs=pl.BlockSpec((tb, d), lambda r, *_: (r, 0)),
        scratch_shapes=[pltpu.VMEM((2, cap, d), F32), pltpu.SemaphoreType.DMA((2,))],
    )
    return pl.pallas_call(
        functools.partial(_combine_kernel, tb=tb, cap=cap),
        grid_spec=grid_spec,
        out_shape=jax.ShapeDtypeStruct((n_tok, d), F32),
        compiler_params=_params("arbitrary", disable_bounds_checks=True),
    )(blk_start.astype(jnp.int32), blk_rows, k_max, entries, x, jnp.broadcast_to(tok_cnt[:, None], (n_tok, LANES)), ye)


def _expert_choice_ffn(x, h, aff, w_gate, w_up, w_down, layer, cfg):
    n_exp = cfg.n_experts
    d = x.shape[1]
    idx_all, gate_all = [], []
    start = 0
    for n_tok in cfg.group_tokens:
        cap = max(1, cfg.ec_capacity * n_tok // n_exp)
        gate, idx = lax.top_k(aff[:, start:start + n_tok], cap)
        idx_all.append(idx + start)
        gate_all.append(gate)
        start += n_tok
    caps = {i.shape[1] for i in idx_all}
    assert len(caps) == 1
    cap = caps.pop()
    idx = jnp.concatenate(idx_all, axis=0)
    gate = jnp.concatenate(gate_all, axis=0)
    xe = h[idx]
    experts = dict(w_base=layer * n_exp, w_period=n_exp, bm=cap)
    hid = _matmul(xe, w_gate, w2=w_up, out_dtype=BF16, bn=cfg.mm_cols, **experts)
    ye = _matmul(hid, w_down, scale=gate[..., None], out_dtype=F32, bn=2 * cfg.mm_cols, **experts)
    return _combine(x, ye.reshape(-1, d), idx.reshape(-1), cfg)


def _trunk(x, cfg, norm_mix, norm_ffn, norm_final, w_attn_in, w_attn_out, attn_sink,
           w_ret_in, w_ret_out, dexp_f, dexp_b, w_router, w_exp_gate, w_exp_up, w_exp_down):
    depth = norm_mix.shape[0]
    w_exp_gate, w_exp_up, w_exp_down = (w.reshape((-1,) + w.shape[2:]) for w in (w_exp_gate, w_exp_up, w_exp_down))
    rows, cols = cfg.mm_rows, cfg.mm_cols
    for i in range(depth):
        h = _rmsnorm(x, norm_mix[i], BF16, cfg.norm_rows)
        j = i // 2
        if i % 2 == 0:
            proj = _dense(h, w_attn_in[j].astype(BF16), out_dtype=BF16, bm=rows, bn=2 * cols)
            mixed = _window_attention(proj, attn_sink[j], cfg)
            x = _dense(mixed, w_attn_out[j].astype(BF16), out_dtype=F32, res=x, bm=rows // 2, bn=2 * cols)
        else:
            proj = _dense(h, w_ret_in[j].astype(BF16), out_dtype=BF16, bm=rows, bn=2 * cols)
            mixed = _retention(proj, dexp_f[j], dexp_b[j], cfg)
            x = _dense(mixed, w_ret_out[j].astype(BF16), out_dtype=F32, res=x, bm=rows // 2, bn=cols)
        h2, aff = _rmsnorm_router(x, norm_ffn[i], w_router[i], cfg.norm_rows)
        x = _expert_choice_ffn(x, h2, aff, w_exp_gate, w_exp_up, w_exp_down, i, cfg)
    return _rmsnorm(x, norm_final, F32, cfg.norm_rows)


def _config(x_prompt, x_sample, att_heads, ret_heads, n_experts):
    b, t, d = x_prompt.shape
    db, dt, _ = x_sample.shape
    return Config(
        d_model=d, att_heads=att_heads, att_kv_heads=att_heads // 4, att_head_dim=d // att_heads,
        att_block=128, ret_heads=ret_heads, ret_dk=d // ret_heads, ret_dv=2 * d // ret_heads,
        ret_chunk=128, n_experts=n_experts, ec_capacity=2,
        seq_lens=(t,) * b + (dt,) * db, group_tokens=(b * t, db * dt),
        norm_rows=256, mm_rows=2048, mm_cols=256, ret_heads_per_step=4, combine_rows=64)


def kernel(x_prompt, x_sample, norm_mix, norm_ffn, norm_final, w_attn_in, w_attn_out, attn_sink,
           w_ret_in, w_ret_out, ret_decay_exp_fwd, ret_decay_exp_bwd,
           w_router, w_exp_gate, w_exp_up, w_exp_down):
    cfg = _config(x_prompt, x_sample, attn_sink.shape[1], ret_decay_exp_fwd.shape[1], w_router.shape[2])
    d = cfg.d_model
    x = jnp.concatenate([x_prompt.reshape(-1, d), x_sample.reshape(-1, d)], axis=0)
    y = _trunk(x, cfg, norm_mix, norm_ffn, norm_final, w_attn_in, w_attn_out, attn_sink,
               w_ret_in, w_ret_out, ret_decay_exp_fwd, ret_decay_exp_bwd,
               w_router, w_exp_gate, w_exp_up, w_exp_down)
    n_prompt = cfg.group_tokens[0]
    return y[:n_prompt].reshape(x_prompt.shape), y[n_prompt:].reshape(x_sample.shape)
```

```python
import functools
from typing import NamedTuple

import numpy as np
import jax
import jax.numpy as jnp
from jax import lax
from jax.experimental import pallas as pl
from jax.experimental.pallas import tpu as pltpu

BF16 = jnp.bfloat16
F32 = jnp.float32
NORM_EPS = 1e-6
NEG_INF = -1e30

V7X_VMEM_BYTES = 64 * 1024 * 1024
VMEM_LIMIT_BYTES = V7X_VMEM_BYTES - 8 * 1024 * 1024
COMBINE_COLS = 512
LANES = 128
COMBINE_WAIT_ROWS = 8


class Config(NamedTuple):
    d_model: int
    att_heads: int
    att_kv_heads: int
    att_head_dim: int
    att_block: int
    ret_heads: int
    ret_dk: int
    ret_dv: int
    ret_chunk: int
    n_experts: int
    ec_capacity: int
    seq_lens: tuple
    group_tokens: tuple
    norm_rows: int
    mm_rows: int
    mm_cols: int
    ret_heads_per_step: int
    combine_rows: int


def _params(*semantics, **kwargs):
    return pltpu.CompilerParams(dimension_semantics=semantics, vmem_limit_bytes=VMEM_LIMIT_BYTES, **kwargs)


def _rmsnorm_f32(x, g):
    return x * lax.rsqrt(jnp.mean(x * x, axis=-1, keepdims=True) + NORM_EPS) * g


def _rmsnorm_kernel(x_ref, g_ref, o_ref):
    o_ref[...] = _rmsnorm_f32(x_ref[...], g_ref[...]).astype(o_ref.dtype)


def _rmsnorm(x, g, out_dtype, rows, row_start=0, n=None):
    d = x.shape[1]
    n = x.shape[0] if n is None else n
    first = row_start // rows
    return pl.pallas_call(
        _rmsnorm_kernel,
        grid=(n // rows,),
        in_specs=[pl.BlockSpec((rows, d), lambda i: (first + i, 0)),
                  pl.BlockSpec((1, d), lambda i: (0, 0))],
        out_specs=pl.BlockSpec((rows, d), lambda i: (i, 0)),
        out_shape=jax.ShapeDtypeStruct((n, d), out_dtype),
        compiler_params=_params("parallel"),
    )(x, g.reshape(1, d))


def _split_bf16(a):
    hi = a.astype(BF16)
    lo = (a - hi.astype(F32)).astype(BF16)
    return hi, lo


def _rmsnorm_router_kernel(x_ref, g_ref, wrt_ref, o_ref, aff_ref):
    h = _rmsnorm_f32(x_ref[...], g_ref[...])
    o_ref[...] = h.astype(o_ref.dtype)
    h_hi, h_lo = _split_bf16(h)
    w_hi, w_lo = _split_bf16(wrt_ref[...])
    dims = (((1,), (1,)), ((), ()))
    logits = (lax.dot_general(w_hi, h_hi, dims, preferred_element_type=F32)
              + lax.dot_general(w_lo, h_hi, dims, preferred_element_type=F32)
              + lax.dot_general(w_hi, h_lo, dims, preferred_element_type=F32))
    m = jnp.max(logits, axis=0, keepdims=True)
    e = jnp.exp(logits - m)
    aff_ref[...] = e / jnp.sum(e, axis=0, keepdims=True)


def _rmsnorm_router(x, g, w_router, rows):
    n, d = x.shape
    n_exp = w_router.shape[1]
    return pl.pallas_call(
        _rmsnorm_router_kernel,
        grid=(n // rows,),
        in_specs=[pl.BlockSpec((rows, d), lambda i: (i, 0)),
                  pl.BlockSpec((1, d), lambda i: (0, 0)),
                  pl.BlockSpec((n_exp, d), lambda i: (0, 0))],
        out_specs=[pl.BlockSpec((rows, d), lambda i: (i, 0)),
                   pl.BlockSpec((n_exp, rows), lambda i: (0, i))],
        out_shape=[jax.ShapeDtypeStruct((n, d), BF16),
                   jax.ShapeDtypeStruct((n_exp, n), F32)],
        compiler_params=_params("parallel"),
    )(x, g.reshape(1, d), w_router.T)


def _mm_kernel(*refs, dual, scaled, residual):
    refs = list(refs)
    o_ref = refs.pop()
    x_ref, w_ref = refs[0], refs[1]
    rest = refs[2:]
    x = x_ref[...]
    acc = jnp.dot(x, w_ref[...].astype(BF16), preferred_element_type=F32)
    if dual:
        up = jnp.dot(x, rest.pop(0)[...].astype(BF16), preferred_element_type=F32)
        acc = acc * jax.nn.sigmoid(acc) * up
    if scaled:
        acc = acc * rest.pop(0)[...]
    if residual:
        acc = acc + rest.pop(0)[...]
    o_ref[...] = acc.astype(o_ref.dtype)


def _matmul(x, w, *, out_dtype, bm, bn, w2=None, scale=None, res=None, w_base=0, w_period=1,
            single_buffer_x=False):
    g_x, m, k = x.shape
    n = w.shape[2]
    grid = (g_x, m // bm, n // bn)
    w_spec = pl.BlockSpec((None, k, bn), lambda g, i, j: (w_base + g % w_period, 0, j))
    x_mode = pl.Buffered(1) if single_buffer_x else None
    in_specs = [pl.BlockSpec((None, bm, k), lambda g, i, j: (g, i, 0), pipeline_mode=x_mode), w_spec]
    args = [x, w]
    if w2 is not None:
        in_specs.append(w_spec)
        args.append(w2)
    if scale is not None:
        in_specs.append(pl.BlockSpec((None, bm, 1), lambda g, i, j: (g, i, 0)))
        args.append(scale)
    if res is not None:
        in_specs.append(pl.BlockSpec((None, bm, bn), lambda g, i, j: (g, i, j)))
        args.append(res)
    kern = functools.partial(_mm_kernel, dual=w2 is not None, scaled=scale is not None,
                             residual=res is not None)
    return pl.pallas_call(
        kern,
        grid=grid,
        in_specs=in_specs,
        out_specs=pl.BlockSpec((None, bm, bn), lambda g, i, j: (g, i, j)),
        out_shape=jax.ShapeDtypeStruct((g_x, m, n), out_dtype),
        compiler_params=_params("parallel", "parallel", "arbitrary"),
    )(*args)


def _dense(x, w, *, out_dtype, bm, bn, res=None):
    out = _matmul(x[None], w[None], out_dtype=out_dtype, bm=bm, bn=bn,
                  res=None if res is None else res[None])
    return out[0]


def _attn_kernel(first_ref, last_ref, sink_ref, q_ref, kp_ref, ko_ref, kn_ref, vp_ref, vo_ref, vn_ref,
                 bias_ref, o_ref, *, kv_heads, group, hd, blk):
    n = pl.program_id(0)
    col = lax.broadcasted_iota(jnp.int32, (1, 3 * blk), 1)
    lo = jnp.where(first_ref[n] == 1, blk, 0)
    hi = jnp.where(last_ref[n] == 1, 2 * blk, 3 * blk)
    drop = (col < lo) | (col >= hi)
    edge = jnp.where(drop, NEG_INF, 0.0).astype(F32)
    scale = hd ** -0.5
    for c in range(kv_heads):
        ks = slice(c * hd, (c + 1) * hd)
        kb = jnp.concatenate([kp_ref[:, ks], ko_ref[:, ks], kn_ref[:, ks]], axis=0)
        vb = jnp.concatenate([vp_ref[:, ks], vo_ref[:, ks], vn_ref[:, ks]], axis=0)
        qs = jnp.concatenate(
            [q_ref[:, (c * group + g) * hd:(c * group + g + 1) * hd] for g in range(group)], axis=0)
        s = lax.dot_general(qs, kb, (((1,), (1,)), ((), ())), preferred_element_type=F32)
        probs = []
        for g in range(group):
            h = c * group + g
            logits = s[g * blk:(g + 1) * blk] * scale + bias_ref[h] + edge
            sink = sink_ref[h]
            m = jnp.maximum(jnp.max(logits, axis=-1, keepdims=True), sink)
            p = jnp.exp(logits - m)
            den = jnp.sum(p, axis=-1, keepdims=True) + jnp.exp(sink - m)
            probs.append((p / den).astype(BF16))
        o = jnp.dot(jnp.concatenate(probs, axis=0), vb, preferred_element_type=F32)
        for g in range(group):
            h = c * group + g
            o_ref[:, h * hd:(h + 1) * hd] = o[g * blk:(g + 1) * blk].astype(o_ref.dtype)


def _block_flags(seq_lens, blk):
    first, last = [], []
    for t in seq_lens:
        nb = t // blk
        first += [1] + [0] * (nb - 1)
        last += [0] * (nb - 1) + [1]
    return jnp.asarray(np.array(first, np.int32)), jnp.asarray(np.array(last, np.int32))


def _alibi_bias(cfg):
    blk = cfg.att_block
    i = np.arange(blk)[:, None]
    j = np.arange(3 * blk)[None, :]
    dist = np.abs((j - blk) - i)
    slopes = jnp.exp2(-8.0 * jnp.arange(1, cfg.att_heads + 1, dtype=F32) / cfg.att_heads)
    bias = -slopes[:, None, None] * jnp.asarray(dist, F32)[None]
    return jnp.where(jnp.asarray(dist <= blk)[None], bias, NEG_INF)


def _window_attention(proj, sink, cfg):
    n_tok = proj.shape[0]
    blk, hd, kvh = cfg.att_block, cfg.att_head_dim, cfg.att_kv_heads
    group = cfg.att_heads // kvh
    qd, kd = cfg.att_heads * hd, kvh * hd
    nb = n_tok // blk
    first, last = _block_flags(cfg.seq_lens, blk)
    k_col, v_col = qd // kd, qd // kd + 1

    def spec(col, shift):
        return pl.BlockSpec((blk, kd), lambda n, *_: (jnp.clip(n + shift, 0, nb - 1), col))

    kern = functools.partial(_attn_kernel, kv_heads=kvh, group=group, hd=hd, blk=blk)
    grid_spec = pltpu.PrefetchScalarGridSpec(
        num_scalar_prefetch=2,
        grid=(nb,),
        in_specs=[pl.BlockSpec(memory_space=pltpu.SMEM),
                  pl.BlockSpec((blk, qd), lambda n, *_: (n, 0)),
                  spec(k_col, -1), spec(k_col, 0), spec(k_col, 1),
                  spec(v_col, -1), spec(v_col, 0), spec(v_col, 1),
                  pl.BlockSpec((cfg.att_heads, blk, 3 * blk), lambda n, *_: (0, 0, 0))],
        out_specs=pl.BlockSpec((blk, qd), lambda n, *_: (n, 0)),
    )
    return pl.pallas_call(
        kern,
        grid_spec=grid_spec,
        out_shape=jax.ShapeDtypeStruct((n_tok, qd), BF16),
        compiler_params=_params("parallel"),
    )(first, last, sink.astype(F32), proj, proj, proj, proj, proj, proj, proj, _alibi_bias(cfg))


def _lane_tile(a, reps):
    return jnp.concatenate([a] * reps, axis=1)


def _ret_kernel(reset_ref, cd_ref, *refs, heads, dk, dv, chunk, final):
    if final:
        q_ref, k_ref, v_ref, xi_ref, zeta_ref, y1_ref, g_ref, o_ref, s_ref = refs
    else:
        q_ref, k_ref, v_ref, xi_ref, zeta_ref, dmat_ref, o_ref, s_ref = refs
    hg = pl.program_id(0)
    n = pl.program_id(1)

    @pl.when(reset_ref[n] == 1)
    def _():
        s_ref[...] = jnp.zeros_like(s_ref)

    for h in range(heads):
        q = q_ref[:, h * dk:(h + 1) * dk]
        k = k_ref[:, h * dk:(h + 1) * dk]
        v = v_ref[:, h * dv:(h + 1) * dv]
        state = s_ref[h]
        y = jnp.dot(q, state.astype(BF16), preferred_element_type=F32) * _lane_tile(xi_ref[h], dv // LANES)
        if final:
            y = y + y1_ref[:, h * dv:(h + 1) * dv]
        else:
            a = lax.dot_general(q, k, (((1,), (1,)), ((), ())), preferred_element_type=F32)
            y = y + jnp.dot((a * dmat_ref[h]).astype(BF16), v, preferred_element_type=F32)
        kz = (k.astype(F32) * _lane_tile(zeta_ref[h], dk // LANES)).astype(BF16)
        s_ref[h] = cd_ref[hg * heads + h] * state + lax.dot_general(
            kz, v, (((0,), (0,)), ((), ())), preferred_element_type=F32)
        if final:
            yn = y * lax.rsqrt(jnp.mean(y * y, axis=-1, keepdims=True) + NORM_EPS)
            gate = g_ref[:, h * dv:(h + 1) * dv].astype(F32)
            o_ref[:, h * dv:(h + 1) * dv] = (yn * (gate * jax.nn.sigmoid(gate))).astype(o_ref.dtype)
        else:
            o_ref[:, h * dv:(h + 1) * dv] = y


def _ret_tables(log_gamma, chunk, dk, backward):
    pos = jnp.arange(chunk, dtype=F32)
    lg = log_gamma[:, None]
    if backward:
        xi = jnp.exp(lg * (chunk - pos))
        zeta = jnp.exp(lg * pos)
    else:
        xi = jnp.exp(lg * (pos + 1.0))
        zeta = jnp.exp(lg * (chunk - 1.0 - pos))
    scale = dk ** -0.5
    ones = jnp.ones((1, 1, LANES), F32)
    xi_t = xi[:, :, None] * ones
    zeta_t = (zeta * scale)[:, :, None] * ones
    cd = jnp.exp(log_gamma * chunk)
    return xi_t, zeta_t, cd


def _ret_intra_decay(lg_f, lg_b, chunk, dk):
    pos = jnp.arange(chunk, dtype=F32)
    rel = pos[:, None] - pos[None, :]
    d_f = jnp.where(rel >= 0, jnp.exp(lg_f[:, None, None] * jnp.maximum(rel, 0.0)), 0.0)
    d_b = jnp.where(rel <= 0, jnp.exp(lg_b[:, None, None] * jnp.maximum(-rel, 0.0)), 0.0)
    return (d_f + d_b) * dk ** -0.5


def _retention(proj, dexp_f, dexp_b, cfg):
    n_tok = proj.shape[0]
    heads, dk, dv, chunk = cfg.ret_heads, cfg.ret_dk, cfg.ret_dv, cfg.ret_chunk
    hps = cfg.ret_heads_per_step
    n_hg = heads // hps
    nc = n_tok // chunk
    first, last = _block_flags(cfg.seq_lens, chunk)
    lg_f = jnp.log1p(-jnp.exp2(-dexp_f.astype(F32)))
    lg_b = jnp.log1p(-jnp.exp2(-dexp_b.astype(F32)))
    xi_f, zeta_f, cd_f = _ret_tables(lg_f, chunk, dk, backward=False)
    xi_b, zeta_b, cd_b = _ret_tables(lg_b, chunk, dk, backward=True)
    dmat = _ret_intra_decay(lg_f, lg_b, chunk, dk)
    k_col0 = heads * dk // (hps * dk)
    v_col0 = 2 * heads * dk // (hps * dv)
    g_col0 = v_col0 + n_hg

    def run(final, reset, cd, xi, zeta, extra_in, extra_specs, out_dtype):
        def tok(n):
            return (nc - 1 - n) if final else n

        tbl = pl.BlockSpec((hps, chunk, LANES), lambda hg, n, *_: (hg, 0, 0))
        in_specs = [pl.BlockSpec(memory_space=pltpu.SMEM),
                    pl.BlockSpec((chunk, hps * dk), lambda hg, n, *_: (tok(n), hg)),
                    pl.BlockSpec((chunk, hps * dk), lambda hg, n, *_: (tok(n), k_col0 + hg)),
                    pl.BlockSpec((chunk, hps * dv), lambda hg, n, *_: (tok(n), v_col0 + hg)),
                    tbl, tbl] + extra_specs(tok)
        kern = functools.partial(_ret_kernel, heads=hps, dk=dk, dv=dv, chunk=chunk, final=final)
        grid_spec = pltpu.PrefetchScalarGridSpec(
            num_scalar_prefetch=1,
            grid=(n_hg, nc),
            in_specs=in_specs,
            out_specs=pl.BlockSpec((chunk, hps * dv), lambda hg, n, *_: (tok(n), hg)),
            scratch_shapes=[pltpu.VMEM((hps, dk, dv), F32)],
        )
        return pl.pallas_call(
            kern,
            grid_spec=grid_spec,
            out_shape=jax.ShapeDtypeStruct((n_tok, heads * dv), out_dtype),
            compiler_params=_params("parallel", "arbitrary"),
        )(reset, cd, proj, proj, proj, xi, zeta, *extra_in)

    y1 = run(False, first, cd_f, xi_f, zeta_f, [dmat],
             lambda tok: [pl.BlockSpec((hps, chunk, chunk), lambda hg, n, *_: (hg, 0, 0))], F32)
    return run(True, last[::-1], cd_b, xi_b, zeta_b, [y1, proj],
               lambda tok: [pl.BlockSpec((chunk, hps * dv), lambda hg, n, *_: (tok(n), hg)),
                            pl.BlockSpec((chunk, hps * dv), lambda hg, n, *_: (tok(n), g_col0 + hg))],
               BF16)


def _combine_kernel(blk_start_ref, n_rows_ref, k_max_ref, ent_ref, x_ref, tok_cnt_ref, ye_ref, o_ref,
                    buf_ref, sem_ref, *, tb, cap):
    r = pl.program_id(0)
    slot = r % 2

    cap_bits = cap.bit_length() - 1

    def start_rows(blk, to_slot):
        base = blk_start_ref[blk]
        n = n_rows_ref[blk]

        def start(ent):
            pltpu.make_async_copy(ye_ref.at[lax.shift_right_logical(ent, cap_bits)],
                                  buf_ref.at[to_slot, ent & (cap - 1)], sem_ref.at[to_slot]).start()

        def body(i, ents):
            following = ent_ref[base + 2 * i + 2], ent_ref[base + 2 * i + 3]
            start(ents[0])
            start(ents[1])
            return following
        last, _ = lax.fori_loop(0, n // 2, body, (ent_ref[base], ent_ref[base + 1]))

        @pl.when(n % 2 == 1)
        def _():
            start(last)

    @pl.when(r == 0)
    def _():
        start_rows(0, 0)

    @pl.when(r + 1 < pl.num_programs(0))
    def _():
        start_rows(r + 1, 1 - slot)

    def wait_rows(rows):
        def body(i, carry):
            pltpu.make_async_copy(ye_ref.at[pl.ds(0, rows)], buf_ref.at[slot, pl.ds(0, rows)],
                                  sem_ref.at[slot]).wait()
            return carry
        return body
    lax.fori_loop(0, n_rows_ref[r] // COMBINE_WAIT_ROWS, wait_rows(COMBINE_WAIT_ROWS), 0)
    lax.fori_loop(0, n_rows_ref[r] % COMBINE_WAIT_ROWS, wait_rows(1), 0)

    tok_cnt = tok_cnt_ref[...]
    k_max = k_max_ref[r]
    slabs = [slice(j * LANES, (j + 1) * LANES) for j in range(x_ref.shape[1] // LANES)]
    per_chunk = COMBINE_COLS // LANES
    for c in range(0, len(slabs), per_chunk):
        chunk = slabs[c:c + per_chunk]

        def add_level(k, accs, chunk=chunk):
            live = k < tok_cnt
            level = pl.ds(pl.multiple_of(k * tb, tb), tb)
            return tuple(a + jnp.where(live, buf_ref[slot, level, s], 0.0) for a, s in zip(accs, chunk))
        sums = lax.fori_loop(0, k_max, add_level, tuple(x_ref[:, s] for s in chunk))
        for s, v in zip(chunk, sums):
            o_ref[:, s] = v


def _token_histogram(tok, n_tok):
    lo_size = 256
    hi_size = pl.cdiv(n_tok, lo_size)
    hi = (tok[:, None] // lo_size == jnp.arange(hi_size, dtype=jnp.int32)[None, :]).astype(BF16)
    lo = (tok[:, None] % lo_size == jnp.arange(lo_size, dtype=jnp.int32)[None, :]).astype(BF16)
    hist = jnp.einsum("rh,rl->hl", hi, lo, preferred_element_type=F32)
    return hist.reshape(-1)[:n_tok].astype(jnp.int32)


def _combine(x, ye, idx, cfg):
    n_tok, d = x.shape
    n_rows = idx.shape[0]
    tb = cfg.combine_rows
    max_k = cfg.n_experts
    cap = tb * max_k
    nblk = n_tok // tb
    pos = jnp.arange(n_rows, dtype=jnp.int32)
    tok_sorted, order = lax.sort((idx, pos), num_keys=1)
    new_tok = jnp.concatenate([jnp.ones((1,), bool), tok_sorted[1:] != tok_sorted[:-1]])
    rank = pos - lax.cummax(jnp.where(new_tok, pos, 0), axis=0)
    entries = order * cap + rank * tb + tok_sorted % tb
    entries = jnp.concatenate([entries, jnp.zeros((3,), jnp.int32)])
    tok_cnt = _token_histogram(idx, n_tok)
    blk_cnt = tok_cnt.reshape(nblk, tb)
    blk_rows = jnp.sum(blk_cnt, axis=1)
    blk_start = jnp.cumsum(blk_rows) - blk_rows
    k_max = jnp.max(blk_cnt, axis=1)

    grid_spec = pltpu.PrefetchScalarGridSpec(
        num_scalar_prefetch=4,
        grid=(nblk,),
        in_specs=[pl.BlockSpec((tb, d), lambda r, *_: (r, 0)),
                  pl.BlockSpec((tb, LANES), lambda r, *_: (r, 0)),
                  pl.BlockSpec(memory_space=pl.ANY)],
        out_specs=pl.BlockSpec((tb, d), lambda r, *_: (r, 0)),
        scratch_shapes=[pltpu.VMEM((2, cap, d), F32), pltpu.SemaphoreType.DMA((2,))],
    )
    return pl.pallas_call(
        functools.partial(_combine_kernel, tb=tb, cap=cap),
        grid_spec=grid_spec,
        out_shape=jax.ShapeDtypeStruct((n_tok, d), F32),
        compiler_params=_params("arbitrary", disable_bounds_checks=True),
    )(blk_start.astype(jnp.int32), blk_rows, k_max, entries, x, jnp.broadcast_to(tok_cnt[:, None], (n_tok, LANES)), ye)


def _expert_choice_ffn(x, h, aff, w_gate, w_up, w_down, layer, cfg):
    n_exp = cfg.n_experts
    d = x.shape[1]
    idx_all, gate_all = [], []
    start = 0
    for n_tok in cfg.group_tokens:
        cap = max(1, cfg.ec_capacity * n_tok // n_exp)
        gate, idx = lax.top_k(aff[:, start:start + n_tok], cap)
        idx_all.append(idx + start)
        gate_all.append(gate)
        start += n_tok
    caps = {i.shape[1] for i in idx_all}
    assert len(caps) == 1
    cap = caps.pop()
    idx = jnp.concatenate(idx_all, axis=0)
    gate = jnp.concatenate(gate_all, axis=0)
    xe = h[idx]
    experts = dict(w_base=layer * n_exp, w_period=n_exp, bm=cap)
    hid = _matmul(xe, w_gate, w2=w_up, out_dtype=BF16, bn=cfg.mm_cols, **experts)
    ye = _matmul(hid, w_down, scale=gate[..., None], out_dtype=F32, bn=2 * cfg.mm_cols,
                 single_buffer_x=True, **experts)
    return _combine(x, ye.reshape(-1, d), idx.reshape(-1), cfg)


def _trunk(x, cfg, norm_mix, norm_ffn, norm_final, w_attn_in, w_attn_out, attn_sink,
           w_ret_in, w_ret_out, dexp_f, dexp_b, w_router, w_exp_gate, w_exp_up, w_exp_down):
    depth = norm_mix.shape[0]
    w_exp_gate, w_exp_up, w_exp_down = (w.reshape((-1,) + w.shape[2:]) for w in (w_exp_gate, w_exp_up, w_exp_down))
    rows, cols = cfg.mm_rows, cfg.mm_cols
    for i in range(depth):
        h = _rmsnorm(x, norm_mix[i], BF16, cfg.norm_rows)
        j = i // 2
        if i % 2 == 0:
            proj = _dense(h, w_attn_in[j].astype(BF16), out_dtype=BF16, bm=rows, bn=2 * cols)
            mixed = _window_attention(proj, attn_sink[j], cfg)
            x = _dense(mixed, w_attn_out[j].astype(BF16), out_dtype=F32, res=x, bm=rows // 2, bn=2 * cols)
        else:
            proj = _dense(h, w_ret_in[j].astype(BF16), out_dtype=BF16, bm=rows, bn=2 * cols)
            mixed = _retention(proj, dexp_f[j], dexp_b[j], cfg)
            x = _dense(mixed, w_ret_out[j].astype(BF16), out_dtype=F32, res=x, bm=rows // 2, bn=cols)
        h2, aff = _rmsnorm_router(x, norm_ffn[i], w_router[i], cfg.norm_rows)
        x = _expert_choice_ffn(x, h2, aff, w_exp_gate, w_exp_up, w_exp_down, i, cfg)
    outs, start = [], 0
    for n_tok in cfg.group_tokens:
        outs.append(_rmsnorm(x, norm_final, F32, cfg.norm_rows, row_start=start, n=n_tok))
        start += n_tok
    return outs


def _config(x_prompt, x_sample, att_heads, ret_heads, n_experts):
    b, t, d = x_prompt.shape
    db, dt, _ = x_sample.shape
    return Config(
        d_model=d, att_heads=att_heads, att_kv_heads=att_heads // 4, att_head_dim=d // att_heads,
        att_block=128, ret_heads=ret_heads, ret_dk=d // ret_heads, ret_dv=2 * d // ret_heads,
        ret_chunk=256, n_experts=n_experts, ec_capacity=2,
        seq_lens=(t,) * b + (dt,) * db, group_tokens=(b * t, db * dt),
        norm_rows=256, mm_rows=2048, mm_cols=256, ret_heads_per_step=4, combine_rows=64)


def kernel(x_prompt, x_sample, norm_mix, norm_ffn, norm_final, w_attn_in, w_attn_out, attn_sink,
           w_ret_in, w_ret_out, ret_decay_exp_fwd, ret_decay_exp_bwd,
           w_router, w_exp_gate, w_exp_up, w_exp_down):
    cfg = _config(x_prompt, x_sample, attn_sink.shape[1], ret_decay_exp_fwd.shape[1], w_router.shape[2])
    d = cfg.d_model
    x = jnp.concatenate([x_prompt.reshape(-1, d), x_sample.reshape(-1, d)], axis=0)
    y_prompt, y_sample = _trunk(x, cfg, norm_mix, norm_ffn, norm_final, w_attn_in, w_attn_out, attn_sink,
                                w_ret_in, w_ret_out, ret_decay_exp_fwd, ret_decay_exp_bwd,
                                w_router, w_exp_gate, w_exp_up, w_exp_down)
    return y_prompt.reshape(x_prompt.shape), y_sample.reshape(x_sample.shape)
```

```python
import functools
from typing import NamedTuple

import numpy as np
import jax
import jax.numpy as jnp
from jax import lax
from jax.experimental import pallas as pl
from jax.experimental.pallas import tpu as pltpu

BF16 = jnp.bfloat16
F32 = jnp.float32
NORM_EPS = 1e-6
NEG_INF = -1e30

V7X_VMEM_BYTES = 64 * 1024 * 1024
VMEM_LIMIT_BYTES = V7X_VMEM_BYTES - 4 * 1024 * 1024
COMBINE_COLS = 512
LANES = 128
COMBINE_WAIT_ROWS = 8
GATHER_UNROLL = 8


class Config(NamedTuple):
    d_model: int
    att_heads: int
    att_kv_heads: int
    att_head_dim: int
    att_block: int
    ret_heads: int
    ret_dk: int
    ret_dv: int
    ret_chunk: int
    n_experts: int
    ec_capacity: int
    seq_lens: tuple
    group_tokens: tuple
    norm_rows: int
    mm_rows: int
    mm_cols: int
    ret_heads_per_step: int
    combine_rows: int


def _params(*semantics, **kwargs):
    return pltpu.CompilerParams(dimension_semantics=semantics, vmem_limit_bytes=VMEM_LIMIT_BYTES, **kwargs)


def _rmsnorm_f32(x, g):
    return x * lax.rsqrt(jnp.mean(x * x, axis=-1, keepdims=True) + NORM_EPS) * g


def _rmsnorm_kernel(x_ref, g_ref, o_ref):
    o_ref[...] = _rmsnorm_f32(x_ref[...], g_ref[...]).astype(o_ref.dtype)


def _rmsnorm(x, g, out_dtype, rows, row_start=0, n=None):
    d = x.shape[1]
    n = x.shape[0] if n is None else n
    first = row_start // rows
    return pl.pallas_call(
        _rmsnorm_kernel,
        grid=(n // rows,),
        in_specs=[pl.BlockSpec((rows, d), lambda i: (first + i, 0)),
                  pl.BlockSpec((1, d), lambda i: (0, 0))],
        out_specs=pl.BlockSpec((rows, d), lambda i: (i, 0)),
        out_shape=jax.ShapeDtypeStruct((n, d), out_dtype),
        compiler_params=_params("parallel"),
    )(x, g.reshape(1, d))


def _split_bf16(a):
    hi = a.astype(BF16)
    lo = (a - hi.astype(F32)).astype(BF16)
    return hi, lo


def _rmsnorm_router_kernel(x_ref, g_ref, wrt_ref, aff_ref):
    h = _rmsnorm_f32(x_ref[...], g_ref[...])
    h_hi, h_lo = _split_bf16(h)
    w_hi, w_lo = _split_bf16(wrt_ref[...])
    dims = (((1,), (1,)), ((), ()))
    logits = (lax.dot_general(w_hi, h_hi, dims, preferred_element_type=F32)
              + lax.dot_general(w_lo, h_hi, dims, preferred_element_type=F32)
              + lax.dot_general(w_hi, h_lo, dims, preferred_element_type=F32))
    m = jnp.max(logits, axis=0, keepdims=True)
    e = jnp.exp(logits - m)
    aff_ref[...] = e / jnp.sum(e, axis=0, keepdims=True)


def _rmsnorm_router(x, g, w_router, rows):
    n, d = x.shape
    n_exp = w_router.shape[1]
    return pl.pallas_call(
        _rmsnorm_router_kernel,
        grid=(n // rows,),
        in_specs=[pl.BlockSpec((rows, d), lambda i: (i, 0)),
                  pl.BlockSpec((1, d), lambda i: (0, 0)),
                  pl.BlockSpec((n_exp, d), lambda i: (0, 0))],
        out_specs=pl.BlockSpec((n_exp, rows), lambda i: (0, i)),
        out_shape=jax.ShapeDtypeStruct((n_exp, n), F32),
        compiler_params=_params("parallel"),
    )(x, g.reshape(1, d), w_router.T)


def _rmsnorm_concat_kernel(xa_ref, xb_ref, g_ref, h_ref, x_ref, *, a_blocks):
    def emit(x):
        x_ref[...] = x
        h_ref[...] = _rmsnorm_f32(x, g_ref[...]).astype(h_ref.dtype)

    @pl.when(pl.program_id(0) < a_blocks)
    def _():
        emit(xa_ref[...])

    @pl.when(pl.program_id(0) >= a_blocks)
    def _():
        emit(xb_ref[...])


def _rmsnorm_concat(xa, xb, g, rows):
    d = xa.shape[1]
    a_blocks, b_blocks = xa.shape[0] // rows, xb.shape[0] // rows
    n = xa.shape[0] + xb.shape[0]
    return pl.pallas_call(
        functools.partial(_rmsnorm_concat_kernel, a_blocks=a_blocks),
        grid=(a_blocks + b_blocks,),
        in_specs=[pl.BlockSpec((rows, d), lambda i: (jnp.minimum(i, a_blocks - 1), 0)),
                  pl.BlockSpec((rows, d), lambda i: (jnp.maximum(i - a_blocks, 0), 0)),
                  pl.BlockSpec((1, d), lambda i: (0, 0))],
        out_specs=[pl.BlockSpec((rows, d), lambda i: (i, 0)),
                   pl.BlockSpec((rows, d), lambda i: (i, 0))],
        out_shape=[jax.ShapeDtypeStruct((n, d), BF16),
                   jax.ShapeDtypeStruct((n, d), F32)],
        compiler_params=_params("parallel"),
    )(xa, xb, g.reshape(1, d))


def _mm_kernel(*refs, dual, scaled, residual):
    refs = list(refs)
    o_ref = refs.pop()
    x_ref, w_ref = refs[0], refs[1]
    rest = refs[2:]
    x = x_ref[...]
    acc = jnp.dot(x, w_ref[...].astype(BF16), preferred_element_type=F32)
    if dual:
        up = jnp.dot(x, rest.pop(0)[...].astype(BF16), preferred_element_type=F32)
        acc = acc * jax.nn.sigmoid(acc) * up
    if scaled:
        acc = acc * rest.pop(0)[...]
    if residual:
        acc = acc + rest.pop(0)[...]
    o_ref[...] = acc.astype(o_ref.dtype)


def _matmul(x, w, *, out_dtype, bm, bn, w2=None, scale=None, res=None, w_base=0, w_period=1,
            single_buffer_x=False):
    g_x, m, k = x.shape
    n = w.shape[2]
    grid = (g_x, m // bm, n // bn)
    w_spec = pl.BlockSpec((None, k, bn), lambda g, i, j: (w_base + g % w_period, 0, j))
    x_mode = pl.Buffered(1) if single_buffer_x else None
    in_specs = [pl.BlockSpec((None, bm, k), lambda g, i, j: (g, i, 0), pipeline_mode=x_mode), w_spec]
    args = [x, w]
    if w2 is not None:
        in_specs.append(w_spec)
        args.append(w2)
    if scale is not None:
        in_specs.append(pl.BlockSpec((None, bm, 1), lambda g, i, j: (g, i, 0)))
        args.append(scale)
    if res is not None:
        in_specs.append(pl.BlockSpec((None, bm, bn), lambda g, i, j: (g, i, j)))
        args.append(res)
    kern = functools.partial(_mm_kernel, dual=w2 is not None, scaled=scale is not None,
                             residual=res is not None)
    return pl.pallas_call(
        kern,
        grid=grid,
        in_specs=in_specs,
        out_specs=pl.BlockSpec((None, bm, bn), lambda g, i, j: (g, i, j)),
        out_shape=jax.ShapeDtypeStruct((g_x, m, n), out_dtype),
        compiler_params=_params("parallel", "parallel", "arbitrary"),
    )(*args)


def _dense(x, w, *, out_dtype, bm, bn, res=None):
    out = _matmul(x[None], w[None], out_dtype=out_dtype, bm=bm, bn=bn,
                  res=None if res is None else res[None])
    return out[0]


def _attn_kernel(first_ref, last_ref, sink_ref, q_ref, kp_ref, ko_ref, kn_ref, vp_ref, vo_ref, vn_ref,
                 bias_ref, o_ref, *, kv_heads, group, hd, blk):
    n = pl.program_id(0)
    col = lax.broadcasted_iota(jnp.int32, (1, 3 * blk), 1)
    lo = jnp.where(first_ref[n] == 1, blk, 0)
    hi = jnp.where(last_ref[n] == 1, 2 * blk, 3 * blk)
    drop = (col < lo) | (col >= hi)
    edge = jnp.where(drop, NEG_INF, 0.0).astype(F32)
    scale = hd ** -0.5
    for c in range(kv_heads):
        ks = slice(c * hd, (c + 1) * hd)
        kb = jnp.concatenate([kp_ref[:, ks], ko_ref[:, ks], kn_ref[:, ks]], axis=0)
        vb = jnp.concatenate([vp_ref[:, ks], vo_ref[:, ks], vn_ref[:, ks]], axis=0)
        qs = jnp.concatenate(
            [q_ref[:, (c * group + g) * hd:(c * group + g + 1) * hd] for g in range(group)], axis=0)
        s = lax.dot_general(qs, kb, (((1,), (1,)), ((), ())), preferred_element_type=F32)
        probs = []
        for g in range(group):
            h = c * group + g
            logits = s[g * blk:(g + 1) * blk] * scale + bias_ref[h] + edge
            sink = sink_ref[h]
            m = jnp.maximum(jnp.max(logits, axis=-1, keepdims=True), sink)
            p = jnp.exp(logits - m)
            den = jnp.sum(p, axis=-1, keepdims=True) + jnp.exp(sink - m)
            probs.append((p / den).astype(BF16))
        o = jnp.dot(jnp.concatenate(probs, axis=0), vb, preferred_element_type=F32)
        for g in range(group):
            h = c * group + g
            o_ref[:, h * hd:(h + 1) * hd] = o[g * blk:(g + 1) * blk].astype(o_ref.dtype)


def _block_flags(seq_lens, blk):
    first, last = [], []
    for t in seq_lens:
        nb = t // blk
        first += [1] + [0] * (nb - 1)
        last += [0] * (nb - 1) + [1]
    return jnp.asarray(np.array(first, np.int32)), jnp.asarray(np.array(last, np.int32))


def _alibi_bias(cfg):
    blk = cfg.att_block
    i = np.arange(blk)[:, None]
    j = np.arange(3 * blk)[None, :]
    dist = np.abs((j - blk) - i)
    slopes = jnp.exp2(-8.0 * jnp.arange(1, cfg.att_heads + 1, dtype=F32) / cfg.att_heads)
    bias = -slopes[:, None, None] * jnp.asarray(dist, F32)[None]
    return jnp.where(jnp.asarray(dist <= blk)[None], bias, NEG_INF)


def _window_attention(proj, sink, cfg):
    n_tok = proj.shape[0]
    blk, hd, kvh = cfg.att_block, cfg.att_head_dim, cfg.att_kv_heads
    group = cfg.att_heads // kvh
    qd, kd = cfg.att_heads * hd, kvh * hd
    nb = n_tok // blk
    first, last = _block_flags(cfg.seq_lens, blk)
    k_col, v_col = qd // kd, qd // kd + 1

    def spec(col, shift):
        return pl.BlockSpec((blk, kd), lambda n, *_: (jnp.clip(n + shift, 0, nb - 1), col))

    kern = functools.partial(_attn_kernel, kv_heads=kvh, group=group, hd=hd, blk=blk)
    grid_spec = pltpu.PrefetchScalarGridSpec(
        num_scalar_prefetch=2,
        grid=(nb,),
        in_specs=[pl.BlockSpec(memory_space=pltpu.SMEM),
                  pl.BlockSpec((blk, qd), lambda n, *_: (n, 0)),
                  spec(k_col, -1), spec(k_col, 0), spec(k_col, 1),
                  spec(v_col, -1), spec(v_col, 0), spec(v_col, 1),
                  pl.BlockSpec((cfg.att_heads, blk, 3 * blk), lambda n, *_: (0, 0, 0))],
        out_specs=pl.BlockSpec((blk, qd), lambda n, *_: (n, 0)),
    )
    return pl.pallas_call(
        kern,
        grid_spec=grid_spec,
        out_shape=jax.ShapeDtypeStruct((n_tok, qd), BF16),
        compiler_params=_params("parallel"),
    )(first, last, sink.astype(F32), proj, proj, proj, proj, proj, proj, proj, _alibi_bias(cfg))


def _lane_tile(a, reps):
    return jnp.concatenate([a] * reps, axis=1)


def _ret_kernel(reset_ref, cd_ref, *refs, heads, dk, dv, chunk, final):
    if final:
        q_ref, k_ref, v_ref, xi_ref, zeta_ref, y1_ref, g_ref, o_ref, s_ref = refs
    else:
        q_ref, k_ref, v_ref, xi_ref, zeta_ref, dmat_ref, o_ref, s_ref = refs
    hg = pl.program_id(0)
    n = pl.program_id(1)

    @pl.when(reset_ref[n] == 1)
    def _():
        s_ref[...] = jnp.zeros_like(s_ref)

    for h in range(heads):
        q = q_ref[:, h * dk:(h + 1) * dk]
        k = k_ref[:, h * dk:(h + 1) * dk]
        v = v_ref[:, h * dv:(h + 1) * dv]
        state = s_ref[h]
        y = jnp.dot(q, state.astype(BF16), preferred_element_type=F32) * _lane_tile(xi_ref[h], dv // LANES)
        if final:
            y = y + y1_ref[:, h * dv:(h + 1) * dv]
        else:
            a = lax.dot_general(q, k, (((1,), (1,)), ((), ())), preferred_element_type=F32)
            y = y + jnp.dot((a * dmat_ref[h]).astype(BF16), v, preferred_element_type=F32)
        kz = (k.astype(F32) * _lane_tile(zeta_ref[h], dk // LANES)).astype(BF16)
        s_ref[h] = cd_ref[hg * heads + h] * state + lax.dot_general(
            kz, v, (((0,), (0,)), ((), ())), preferred_element_type=F32)
        if final:
            yn = y * lax.rsqrt(jnp.mean(y * y, axis=-1, keepdims=True) + NORM_EPS)
            gate = g_ref[:, h * dv:(h + 1) * dv].astype(F32)
            o_ref[:, h * dv:(h + 1) * dv] = (yn * (gate * jax.nn.sigmoid(gate))).astype(o_ref.dtype)
        else:
            o_ref[:, h * dv:(h + 1) * dv] = y


def _ret_tables(log_gamma, chunk, dk, backward):
    pos = jnp.arange(chunk, dtype=F32)
    lg = log_gamma[:, None]
    if backward:
        xi = jnp.exp(lg * (chunk - pos))
        zeta = jnp.exp(lg * pos)
    else:
        xi = jnp.exp(lg * (pos + 1.0))
        zeta = jnp.exp(lg * (chunk - 1.0 - pos))
    scale = dk ** -0.5
    ones = jnp.ones((1, 1, LANES), F32)
    xi_t = xi[:, :, None] * ones
    zeta_t = (zeta * scale)[:, :, None] * ones
    cd = jnp.exp(log_gamma * chunk)
    return xi_t, zeta_t, cd


def _ret_intra_decay(lg_f, lg_b, chunk, dk):
    pos = jnp.arange(chunk, dtype=F32)
    rel = pos[:, None] - pos[None, :]
    d_f = jnp.where(rel >= 0, jnp.exp(lg_f[:, None, None] * jnp.maximum(rel, 0.0)), 0.0)
    d_b = jnp.where(rel <= 0, jnp.exp(lg_b[:, None, None] * jnp.maximum(-rel, 0.0)), 0.0)
    return (d_f + d_b) * dk ** -0.5


def _retention(proj, dexp_f, dexp_b, cfg):
    n_tok = proj.shape[0]
    heads, dk, dv, chunk = cfg.ret_heads, cfg.ret_dk, cfg.ret_dv, cfg.ret_chunk
    hps = cfg.ret_heads_per_step
    n_hg = heads // hps
    nc = n_tok // chunk
    first, last = _block_flags(cfg.seq_lens, chunk)
    lg_f = jnp.log1p(-jnp.exp2(-dexp_f.astype(F32)))
    lg_b = jnp.log1p(-jnp.exp2(-dexp_b.astype(F32)))
    xi_f, zeta_f, cd_f = _ret_tables(lg_f, chunk, dk, backward=False)
    xi_b, zeta_b, cd_b = _ret_tables(lg_b, chunk, dk, backward=True)
    dmat = _ret_intra_decay(lg_f, lg_b, chunk, dk)
    k_col0 = heads * dk // (hps * dk)
    v_col0 = 2 * heads * dk // (hps * dv)
    g_col0 = v_col0 + n_hg

    def run(final, reset, cd, xi, zeta, extra_in, extra_specs, out_dtype):
        def tok(n):
            return (nc - 1 - n) if final else n

        tbl = pl.BlockSpec((hps, chunk, LANES), lambda hg, n, *_: (hg, 0, 0))
        in_specs = [pl.BlockSpec(memory_space=pltpu.SMEM),
                    pl.BlockSpec((chunk, hps * dk), lambda hg, n, *_: (tok(n), hg)),
                    pl.BlockSpec((chunk, hps * dk), lambda hg, n, *_: (tok(n), k_col0 + hg)),
                    pl.BlockSpec((chunk, hps * dv), lambda hg, n, *_: (tok(n), v_col0 + hg)),
                    tbl, tbl] + extra_specs(tok)
        kern = functools.partial(_ret_kernel, heads=hps, dk=dk, dv=dv, chunk=chunk, final=final)
        grid_spec = pltpu.PrefetchScalarGridSpec(
            num_scalar_prefetch=1,
            grid=(n_hg, nc),
            in_specs=in_specs,
            out_specs=pl.BlockSpec((chunk, hps * dv), lambda hg, n, *_: (tok(n), hg)),
            scratch_shapes=[pltpu.VMEM((hps, dk, dv), F32)],
        )
        return pl.pallas_call(
            kern,
            grid_spec=grid_spec,
            out_shape=jax.ShapeDtypeStruct((n_tok, heads * dv), out_dtype),
            compiler_params=_params("parallel", "arbitrary"),
        )(reset, cd, proj, proj, proj, xi, zeta, *extra_in)

    y1 = run(False, first, cd_f, xi_f, zeta_f, [dmat],
             lambda tok: [pl.BlockSpec((hps, chunk, chunk), lambda hg, n, *_: (hg, 0, 0))], F32)
    return run(True, last[::-1], cd_b, xi_b, zeta_b, [y1, proj],
               lambda tok: [pl.BlockSpec((chunk, hps * dv), lambda hg, n, *_: (tok(n), hg)),
                            pl.BlockSpec((chunk, hps * dv), lambda hg, n, *_: (tok(n), g_col0 + hg))],
               BF16)


def _combine_kernel(blk_start_ref, n_rows_ref, k_max_ref, ent_ref, x_ref, tok_cnt_ref, ye_ref, o_ref,
                    buf_ref, sem_ref, *, tb, cap):
    r = pl.program_id(0)
    slot = r % 2

    cap_bits = cap.bit_length() - 1

    def start_rows(blk, to_slot):
        base = blk_start_ref[blk]
        n = n_rows_ref[blk]

        def start(ent):
            pltpu.make_async_copy(ye_ref.at[lax.shift_right_logical(ent, cap_bits)],
                                  buf_ref.at[to_slot, ent & (cap - 1)], sem_ref.at[to_slot]).start()

        def body(i, ents):
            following = ent_ref[base + 2 * i + 2], ent_ref[base + 2 * i + 3]
            start(ents[0])
            start(ents[1])
            return following
        last, _ = lax.fori_loop(0, n // 2, body, (ent_ref[base], ent_ref[base + 1]))

        @pl.when(n % 2 == 1)
        def _():
            start(last)

    @pl.when(r == 0)
    def _():
        start_rows(0, 0)

    @pl.when(r + 1 < pl.num_programs(0))
    def _():
        start_rows(r + 1, 1 - slot)

    def wait_rows(rows):
        def body(i, carry):
            pltpu.make_async_copy(ye_ref.at[pl.ds(0, rows)], buf_ref.at[slot, pl.ds(0, rows)],
                                  sem_ref.at[slot]).wait()
            return carry
        return body
    lax.fori_loop(0, n_rows_ref[r] // COMBINE_WAIT_ROWS, wait_rows(COMBINE_WAIT_ROWS), 0)
    lax.fori_loop(0, n_rows_ref[r] % COMBINE_WAIT_ROWS, wait_rows(1), 0)

    tok_cnt = tok_cnt_ref[...]
    k_max = k_max_ref[r]
    slabs = [slice(j * LANES, (j + 1) * LANES) for j in range(x_ref.shape[1] // LANES)]
    per_chunk = COMBINE_COLS // LANES
    for c in range(0, len(slabs), per_chunk):
        chunk = slabs[c:c + per_chunk]

        def add_level(k, accs, chunk=chunk):
            live = k < tok_cnt
            level = pl.ds(pl.multiple_of(k * tb, tb), tb)
            return tuple(a + jnp.where(live, buf_ref[slot, level, s], 0.0) for a, s in zip(accs, chunk))
        sums = lax.fori_loop(0, k_max, add_level, tuple(x_ref[:, s] for s in chunk))
        for s, v in zip(chunk, sums):
            o_ref[:, s] = v


def _token_histogram(tok, n_tok):
    lo_size = 256
    hi_size = pl.cdiv(n_tok, lo_size)
    hi = (tok[:, None] // lo_size == jnp.arange(hi_size, dtype=jnp.int32)[None, :]).astype(BF16)
    lo = (tok[:, None] % lo_size == jnp.arange(lo_size, dtype=jnp.int32)[None, :]).astype(BF16)
    hist = jnp.einsum("rh,rl->hl", hi, lo, preferred_element_type=F32)
    return hist.reshape(-1)[:n_tok].astype(jnp.int32)


def _combine(x, ye, idx, cfg):
    n_tok, d = x.shape
    n_rows = idx.shape[0]
    tb = cfg.combine_rows
    max_k = cfg.n_experts
    cap = tb * max_k
    nblk = n_tok // tb
    pos = jnp.arange(n_rows, dtype=jnp.int32)
    tok_sorted, order = lax.sort((idx, pos), num_keys=1)
    new_tok = jnp.concatenate([jnp.ones((1,), bool), tok_sorted[1:] != tok_sorted[:-1]])
    rank = pos - lax.cummax(jnp.where(new_tok, pos, 0), axis=0)
    entries = order * cap + rank * tb + tok_sorted % tb
    entries = jnp.concatenate([entries, jnp.zeros((3,), jnp.int32)])
    tok_cnt = _token_histogram(idx, n_tok)
    blk_cnt = tok_cnt.reshape(nblk, tb)
    blk_rows = jnp.sum(blk_cnt, axis=1)
    blk_start = jnp.cumsum(blk_rows) - blk_rows
    k_max = jnp.max(blk_cnt, axis=1)

    grid_spec = pltpu.PrefetchScalarGridSpec(
        num_scalar_prefetch=4,
        grid=(nblk,),
        in_specs=[pl.BlockSpec((tb, d), lambda r, *_: (r, 0)),
                  pl.BlockSpec((tb, LANES), lambda r, *_: (r, 0)),
                  pl.BlockSpec(memory_space=pl.ANY)],
        out_specs=pl.BlockSpec((tb, d), lambda r, *_: (r, 0)),
        scratch_shapes=[pltpu.VMEM((2, cap, d), F32), pltpu.SemaphoreType.DMA((2,))],
    )
    return pl.pallas_call(
        functools.partial(_combine_kernel, tb=tb, cap=cap),
        grid_spec=grid_spec,
        out_shape=jax.ShapeDtypeStruct((n_tok, d), F32),
        compiler_params=_params("arbitrary", disable_bounds_checks=True),
    )(blk_start.astype(jnp.int32), blk_rows, k_max, entries, x, jnp.broadcast_to(tok_cnt[:, None], (n_tok, LANES)), ye)


def _gather_norm_kernel(idx_ref, x_ref, g_ref, o_ref, buf_ref, sem_ref, *, rb):
    s = pl.program_id(0)
    slot = s % 2

    def start_rows(blk, to_slot):
        base = blk * rb

        def body(i, carry):
            pltpu.make_async_copy(x_ref.at[idx_ref[base + i]], buf_ref.at[to_slot, i], sem_ref.at[to_slot]).start()
            return carry
        lax.fori_loop(0, rb, body, 0, unroll=GATHER_UNROLL)

    @pl.when(s == 0)
    def _():
        start_rows(0, 0)

    @pl.when(s + 1 < pl.num_programs(0))
    def _():
        start_rows(s + 1, 1 - slot)

    def wait_rows(i, carry):
        pltpu.make_async_copy(x_ref.at[pl.ds(0, COMBINE_WAIT_ROWS)], buf_ref.at[slot, pl.ds(0, COMBINE_WAIT_ROWS)],
                              sem_ref.at[slot]).wait()
        return carry
    lax.fori_loop(0, rb // COMBINE_WAIT_ROWS, wait_rows, 0)
    o_ref[...] = _rmsnorm_f32(buf_ref[slot], g_ref[...]).astype(o_ref.dtype)


def _gather_norm(x, g, idx, cfg):
    n_rows = idx.shape[0]
    d = x.shape[1]
    rb = cfg.norm_rows
    grid_spec = pltpu.PrefetchScalarGridSpec(
        num_scalar_prefetch=1,
        grid=(n_rows // rb,),
        in_specs=[pl.BlockSpec(memory_space=pl.ANY),
                  pl.BlockSpec((1, d), lambda s, *_: (0, 0))],
        out_specs=pl.BlockSpec((rb, d), lambda s, *_: (s, 0)),
        scratch_shapes=[pltpu.VMEM((2, rb, d), F32), pltpu.SemaphoreType.DMA((2,))],
    )
    return pl.pallas_call(
        functools.partial(_gather_norm_kernel, rb=rb),
        grid_spec=grid_spec,
        out_shape=jax.ShapeDtypeStruct((n_rows, d), BF16),
        compiler_params=_params("arbitrary", disable_bounds_checks=True),
    )(idx, x, g.reshape(1, d))


def _expert_choice_ffn(x, norm_g, aff, w_gate, w_up, w_down, layer, cfg):
    n_exp = cfg.n_experts
    d = x.shape[1]
    idx_all, gate_all = [], []
    start = 0
    for n_tok in cfg.group_tokens:
        cap = max(1, cfg.ec_capacity * n_tok // n_exp)
        gate, idx = lax.top_k(aff[:, start:start + n_tok], cap)
        idx_all.append(idx + start)
        gate_all.append(gate)
        start += n_tok
    caps = {i.shape[1] for i in idx_all}
    assert len(caps) == 1
    cap = caps.pop()
    idx = jnp.concatenate(idx_all, axis=0)
    gate = jnp.concatenate(gate_all, axis=0)
    xe = _gather_norm(x, norm_g, idx.reshape(-1), cfg).reshape(idx.shape + (d,))
    experts = dict(w_base=layer * n_exp, w_period=n_exp, bm=cap, bn=cfg.mm_cols)
    hid = _matmul(xe, w_gate, w2=w_up, out_dtype=BF16, **experts)
    ye = _matmul(hid, w_down, scale=gate[..., None], out_dtype=F32, **experts)
    return _combine(x, ye.reshape(-1, d), idx.reshape(-1), cfg)


def _trunk(x_groups, cfg, norm_mix, norm_ffn, norm_final, w_attn_in, w_attn_out, attn_sink,
           w_ret_in, w_ret_out, dexp_f, dexp_b, w_router, w_exp_gate, w_exp_up, w_exp_down):
    depth = norm_mix.shape[0]
    w_exp_gate, w_exp_up, w_exp_down = (w.reshape((-1,) + w.shape[2:]) for w in (w_exp_gate, w_exp_up, w_exp_down))
    rows, cols = cfg.mm_rows, cfg.mm_cols
    for i in range(depth):
        if i == 0:
            h, x = _rmsnorm_concat(*x_groups, norm_mix[0], cfg.norm_rows)
        else:
            h = _rmsnorm(x, norm_mix[i], BF16, cfg.norm_rows)
        j = i // 2
        if i % 2 == 0:
            proj = _dense(h, w_attn_in[j].astype(BF16), out_dtype=BF16, bm=rows, bn=2 * cols)
            mixed = _window_attention(proj, attn_sink[j], cfg)
            x = _dense(mixed, w_attn_out[j].astype(BF16), out_dtype=F32, res=x, bm=rows // 2, bn=2 * cols)
        else:
            proj = _dense(h, w_ret_in[j].astype(BF16), out_dtype=BF16, bm=rows, bn=2 * cols)
            mixed = _retention(proj, dexp_f[j], dexp_b[j], cfg)
            x = _dense(mixed, w_ret_out[j].astype(BF16), out_dtype=F32, res=x, bm=rows // 2, bn=cols)
        aff = _rmsnorm_router(x, norm_ffn[i], w_router[i], cfg.norm_rows)
        x = _expert_choice_ffn(x, norm_ffn[i], aff, w_exp_gate, w_exp_up, w_exp_down, i, cfg)
    outs, start = [], 0
    for n_tok in cfg.group_tokens:
        outs.append(_rmsnorm(x, norm_final, F32, cfg.norm_rows, row_start=start, n=n_tok))
        start += n_tok
    return outs


def _config(x_prompt, x_sample, att_heads, ret_heads, n_experts):
    b, t, d = x_prompt.shape
    db, dt, _ = x_sample.shape
    return Config(
        d_model=d, att_heads=att_heads, att_kv_heads=att_heads // 4, att_head_dim=d // att_heads,
        att_block=128, ret_heads=ret_heads, ret_dk=d // ret_heads, ret_dv=2 * d // ret_heads,
        ret_chunk=256, n_experts=n_experts, ec_capacity=2,
        seq_lens=(t,) * b + (dt,) * db, group_tokens=(b * t, db * dt),
        norm_rows=256, mm_rows=2048, mm_cols=256, ret_heads_per_step=4, combine_rows=64)


def kernel(x_prompt, x_sample, norm_mix, norm_ffn, norm_final, w_attn_in, w_attn_out, attn_sink,
           w_ret_in, w_ret_out, ret_decay_exp_fwd, ret_decay_exp_bwd,
           w_router, w_exp_gate, w_exp_up, w_exp_down):
    cfg = _config(x_prompt, x_sample, attn_sink.shape[1], ret_decay_exp_fwd.shape[1], w_router.shape[2])
    d = cfg.d_model
    x_groups = (x_prompt.reshape(-1, d), x_sample.reshape(-1, d))
    y_prompt, y_sample = _trunk(x_groups, cfg, norm_mix, norm_ffn, norm_final, w_attn_in, w_attn_out, attn_sink,
                                w_ret_in, w_ret_out, ret_decay_exp_fwd, ret_decay_exp_bwd,
                                w_router, w_exp_gate, w_exp_up, w_exp_down)
    return y_prompt.reshape(x_prompt.shape), y_sample.reshape(x_sample.shape)
```

```python
import functools
from typing import NamedTuple

import numpy as np
import jax
import jax.numpy as jnp
from jax import lax
from jax.experimental import pallas as pl
from jax.experimental.pallas import tpu as pltpu

BF16 = jnp.bfloat16
F32 = jnp.float32
NORM_EPS = 1e-6
NEG_INF = -1e30

V7X_VMEM_BYTES = 64 * 1024 * 1024
VMEM_LIMIT_BYTES = V7X_VMEM_BYTES - 4 * 1024 * 1024
COMBINE_COLS = 512
LANES = 128
COMBINE_WAIT_ROWS = 8
GATHER_UNROLL = 8


class Config(NamedTuple):
    d_model: int
    att_heads: int
    att_kv_heads: int
    att_head_dim: int
    att_block: int
    ret_heads: int
    ret_dk: int
    ret_dv: int
    ret_chunk: int
    n_experts: int
    ec_capacity: int
    seq_lens: tuple
    group_tokens: tuple
    norm_rows: int
    mm_rows: int
    mm_cols: int
    ret_heads_per_step: int
    combine_rows: int


def _params(*semantics, **kwargs):
    return pltpu.CompilerParams(dimension_semantics=semantics, vmem_limit_bytes=VMEM_LIMIT_BYTES, **kwargs)


def _rmsnorm_f32(x, g):
    return x * lax.rsqrt(jnp.mean(x * x, axis=-1, keepdims=True) + NORM_EPS) * g


def _split_bf16(a):
    hi = a.astype(BF16)
    lo = (a - hi.astype(F32)).astype(BF16)
    return hi, lo


def _rmsnorm_router_kernel(x_ref, g_ref, wrt_ref, aff_ref):
    h = _rmsnorm_f32(x_ref[...], g_ref[...])
    h_hi, h_lo = _split_bf16(h)
    w_hi, w_lo = _split_bf16(wrt_ref[...])
    dims = (((1,), (1,)), ((), ()))
    logits = (lax.dot_general(w_hi, h_hi, dims, preferred_element_type=F32)
              + lax.dot_general(w_lo, h_hi, dims, preferred_element_type=F32)
              + lax.dot_general(w_hi, h_lo, dims, preferred_element_type=F32))
    m = jnp.max(logits, axis=0, keepdims=True)
    e = jnp.exp(logits - m)
    aff_ref[...] = e / jnp.sum(e, axis=0, keepdims=True)


def _rmsnorm_router(x, g, w_router, rows):
    n, d = x.shape
    n_exp = w_router.shape[1]
    return pl.pallas_call(
        _rmsnorm_router_kernel,
        grid=(n // rows,),
        in_specs=[pl.BlockSpec((rows, d), lambda i: (i, 0)),
                  pl.BlockSpec((1, d), lambda i: (0, 0)),
                  pl.BlockSpec((n_exp, d), lambda i: (0, 0))],
        out_specs=pl.BlockSpec((n_exp, rows), lambda i: (0, i)),
        out_shape=jax.ShapeDtypeStruct((n_exp, n), F32),
        compiler_params=_params("parallel"),
    )(x, g.reshape(1, d), w_router.T)


def _rmsnorm_concat_kernel(xa_ref, xb_ref, g_ref, h_ref, x_ref, *, a_blocks):
    def emit(x):
        x_ref[...] = x
        h_ref[...] = _rmsnorm_f32(x, g_ref[...]).astype(h_ref.dtype)

    @pl.when(pl.program_id(0) < a_blocks)
    def _():
        emit(xa_ref[...])

    @pl.when(pl.program_id(0) >= a_blocks)
    def _():
        emit(xb_ref[...])


def _rmsnorm_concat(xa, xb, g, rows):
    d = xa.shape[1]
    a_blocks, b_blocks = xa.shape[0] // rows, xb.shape[0] // rows
    n = xa.shape[0] + xb.shape[0]
    return pl.pallas_call(
        functools.partial(_rmsnorm_concat_kernel, a_blocks=a_blocks),
        grid=(a_blocks + b_blocks,),
        in_specs=[pl.BlockSpec((rows, d), lambda i: (jnp.minimum(i, a_blocks - 1), 0)),
                  pl.BlockSpec((rows, d), lambda i: (jnp.maximum(i - a_blocks, 0), 0)),
                  pl.BlockSpec((1, d), lambda i: (0, 0))],
        out_specs=[pl.BlockSpec((rows, d), lambda i: (i, 0)),
                   pl.BlockSpec((rows, d), lambda i: (i, 0))],
        out_shape=[jax.ShapeDtypeStruct((n, d), BF16),
                   jax.ShapeDtypeStruct((n, d), F32)],
        compiler_params=_params("parallel"),
    )(xa, xb, g.reshape(1, d))


def _mm_kernel(*refs, dual, scaled, residual):
    refs = list(refs)
    o_ref = refs.pop()
    x_ref, w_ref = refs[0], refs[1]
    rest = refs[2:]
    x = x_ref[...]
    acc = jnp.dot(x, w_ref[...].astype(BF16), preferred_element_type=F32)
    if dual:
        up = jnp.dot(x, rest.pop(0)[...].astype(BF16), preferred_element_type=F32)
        acc = acc * jax.nn.sigmoid(acc) * up
    if scaled:
        acc = acc * rest.pop(0)[...]
    if residual:
        acc = acc + rest.pop(0)[...]
    o_ref[...] = acc.astype(o_ref.dtype)


def _matmul(x, w, *, out_dtype, bm, bn, w2=None, scale=None, res=None, w_base=0, w_period=1,
            single_buffer_x=False):
    g_x, m, k = x.shape
    n = w.shape[2]
    grid = (g_x, m // bm, n // bn)
    w_spec = pl.BlockSpec((None, k, bn), lambda g, i, j: (w_base + g % w_period, 0, j))
    x_mode = pl.Buffered(1) if single_buffer_x else None
    in_specs = [pl.BlockSpec((None, bm, k), lambda g, i, j: (g, i, 0), pipeline_mode=x_mode), w_spec]
    args = [x, w]
    if w2 is not None:
        in_specs.append(w_spec)
        args.append(w2)
    if scale is not None:
        in_specs.append(pl.BlockSpec((None, bm, 1), lambda g, i, j: (g, i, 0)))
        args.append(scale)
    if res is not None:
        in_specs.append(pl.BlockSpec((None, bm, bn), lambda g, i, j: (g, i, j)))
        args.append(res)
    kern = functools.partial(_mm_kernel, dual=w2 is not None, scaled=scale is not None,
                             residual=res is not None)
    return pl.pallas_call(
        kern,
        grid=grid,
        in_specs=in_specs,
        out_specs=pl.BlockSpec((None, bm, bn), lambda g, i, j: (g, i, j)),
        out_shape=jax.ShapeDtypeStruct((g_x, m, n), out_dtype),
        compiler_params=_params("parallel", "parallel", "arbitrary"),
    )(*args)


def _dense(x, w, *, out_dtype, bm, bn, res=None):
    out = _matmul(x[None], w[None], out_dtype=out_dtype, bm=bm, bn=bn,
                  res=None if res is None else res[None])
    return out[0]


def _attn_kernel(first_ref, last_ref, sink_ref, q_ref, kp_ref, ko_ref, kn_ref, vp_ref, vo_ref, vn_ref,
                 bias_ref, o_ref, *, kv_heads, group, hd, blk):
    n = pl.program_id(0)
    col = lax.broadcasted_iota(jnp.int32, (1, 3 * blk), 1)
    lo = jnp.where(first_ref[n] == 1, blk, 0)
    hi = jnp.where(last_ref[n] == 1, 2 * blk, 3 * blk)
    drop = (col < lo) | (col >= hi)
    edge = jnp.where(drop, NEG_INF, 0.0).astype(F32)
    scale = hd ** -0.5
    for c in range(kv_heads):
        ks = slice(c * hd, (c + 1) * hd)
        kb = jnp.concatenate([kp_ref[:, ks], ko_ref[:, ks], kn_ref[:, ks]], axis=0)
        vb = jnp.concatenate([vp_ref[:, ks], vo_ref[:, ks], vn_ref[:, ks]], axis=0)
        qs = jnp.concatenate(
            [q_ref[:, (c * group + g) * hd:(c * group + g + 1) * hd] for g in range(group)], axis=0)
        s = lax.dot_general(qs, kb, (((1,), (1,)), ((), ())), preferred_element_type=F32)
        probs = []
        for g in range(group):
            h = c * group + g
            logits = s[g * blk:(g + 1) * blk] * scale + bias_ref[h] + edge
            sink = sink_ref[h]
            m = jnp.maximum(jnp.max(logits, axis=-1, keepdims=True), sink)
            p = jnp.exp(logits - m)
            den = jnp.sum(p, axis=-1, keepdims=True) + jnp.exp(sink - m)
            probs.append((p / den).astype(BF16))
        o = jnp.dot(jnp.concatenate(probs, axis=0), vb, preferred_element_type=F32)
        for g in range(group):
            h = c * group + g
            o_ref[:, h * hd:(h + 1) * hd] = o[g * blk:(g + 1) * blk].astype(o_ref.dtype)


def _block_flags(seq_lens, blk):
    first, last = [], []
    for t in seq_lens:
        nb = t // blk
        first += [1] + [0] * (nb - 1)
        last += [0] * (nb - 1) + [1]
    return jnp.asarray(np.array(first, np.int32)), jnp.asarray(np.array(last, np.int32))


def _alibi_bias(cfg):
    blk = cfg.att_block
    i = np.arange(blk)[:, None]
    j = np.arange(3 * blk)[None, :]
    dist = np.abs((j - blk) - i)
    slopes = jnp.exp2(-8.0 * jnp.arange(1, cfg.att_heads + 1, dtype=F32) / cfg.att_heads)
    bias = -slopes[:, None, None] * jnp.asarray(dist, F32)[None]
    return jnp.where(jnp.asarray(dist <= blk)[None], bias, NEG_INF)


def _window_attention(proj, sink, cfg):
    n_tok = proj.shape[0]
    blk, hd, kvh = cfg.att_block, cfg.att_head_dim, cfg.att_kv_heads
    group = cfg.att_heads // kvh
    qd, kd = cfg.att_heads * hd, kvh * hd
    nb = n_tok // blk
    first, last = _block_flags(cfg.seq_lens, blk)
    k_col, v_col = qd // kd, qd // kd + 1

    def spec(col, shift):
        return pl.BlockSpec((blk, kd), lambda n, *_: (jnp.clip(n + shift, 0, nb - 1), col))

    kern = functools.partial(_attn_kernel, kv_heads=kvh, group=group, hd=hd, blk=blk)
    grid_spec = pltpu.PrefetchScalarGridSpec(
        num_scalar_prefetch=2,
        grid=(nb,),
        in_specs=[pl.BlockSpec(memory_space=pltpu.SMEM),
                  pl.BlockSpec((blk, qd), lambda n, *_: (n, 0)),
                  spec(k_col, -1), spec(k_col, 0), spec(k_col, 1),
                  spec(v_col, -1), spec(v_col, 0), spec(v_col, 1),
                  pl.BlockSpec((cfg.att_heads, blk, 3 * blk), lambda n, *_: (0, 0, 0))],
        out_specs=pl.BlockSpec((blk, qd), lambda n, *_: (n, 0)),
    )
    return pl.pallas_call(
        kern,
        grid_spec=grid_spec,
        out_shape=jax.ShapeDtypeStruct((n_tok, qd), BF16),
        compiler_params=_params("parallel"),
    )(first, last, sink.astype(F32), proj, proj, proj, proj, proj, proj, proj, _alibi_bias(cfg))


def _lane_tile(a, reps):
    return jnp.concatenate([a] * reps, axis=1)


def _ret_kernel(reset_ref, cd_ref, *refs, heads, dk, dv, chunk, final):
    if final:
        q_ref, k_ref, v_ref, xi_ref, zeta_ref, y1_ref, g_ref, o_ref, s_ref = refs
    else:
        q_ref, k_ref, v_ref, xi_ref, zeta_ref, dmat_ref, o_ref, s_ref = refs
    hg = pl.program_id(0)
    n = pl.program_id(1)

    @pl.when(reset_ref[n] == 1)
    def _():
        s_ref[...] = jnp.zeros_like(s_ref)

    for h in range(heads):
        q = q_ref[:, h * dk:(h + 1) * dk]
        k = k_ref[:, h * dk:(h + 1) * dk]
        v = v_ref[:, h * dv:(h + 1) * dv]
        state = s_ref[h]
        y = jnp.dot(q, state.astype(BF16), preferred_element_type=F32) * _lane_tile(xi_ref[h], dv // LANES)
        if final:
            y = y + y1_ref[:, h * dv:(h + 1) * dv]
        else:
            a = lax.dot_general(q, k, (((1,), (1,)), ((), ())), preferred_element_type=F32)
            y = y + jnp.dot((a * dmat_ref[h]).astype(BF16), v, preferred_element_type=F32)
        kz = (k.astype(F32) * _lane_tile(zeta_ref[h], dk // LANES)).astype(BF16)
        s_ref[h] = cd_ref[hg * heads + h] * state + lax.dot_general(
            kz, v, (((0,), (0,)), ((), ())), preferred_element_type=F32)
        if final:
            yn = y * lax.rsqrt(jnp.mean(y * y, axis=-1, keepdims=True) + NORM_EPS)
            gate = g_ref[:, h * dv:(h + 1) * dv].astype(F32)
            o_ref[:, h * dv:(h + 1) * dv] = (yn * (gate * jax.nn.sigmoid(gate))).astype(o_ref.dtype)
        else:
            o_ref[:, h * dv:(h + 1) * dv] = y


def _ret_tables(log_gamma, chunk, dk, backward):
    pos = jnp.arange(chunk, dtype=F32)
    lg = log_gamma[:, None]
    if backward:
        xi = jnp.exp(lg * (chunk - pos))
        zeta = jnp.exp(lg * pos)
    else:
        xi = jnp.exp(lg * (pos + 1.0))
        zeta = jnp.exp(lg * (chunk - 1.0 - pos))
    scale = dk ** -0.5
    ones = jnp.ones((1, 1, LANES), F32)
    xi_t = xi[:, :, None] * ones
    zeta_t = (zeta * scale)[:, :, None] * ones
    cd = jnp.exp(log_gamma * chunk)
    return xi_t, zeta_t, cd


def _ret_intra_decay(lg_f, lg_b, chunk, dk):
    pos = jnp.arange(chunk, dtype=F32)
    rel = pos[:, None] - pos[None, :]
    d_f = jnp.where(rel >= 0, jnp.exp(lg_f[:, None, None] * jnp.maximum(rel, 0.0)), 0.0)
    d_b = jnp.where(rel <= 0, jnp.exp(lg_b[:, None, None] * jnp.maximum(-rel, 0.0)), 0.0)
    return (d_f + d_b) * dk ** -0.5


def _retention(proj, dexp_f, dexp_b, cfg):
    n_tok = proj.shape[0]
    heads, dk, dv, chunk = cfg.ret_heads, cfg.ret_dk, cfg.ret_dv, cfg.ret_chunk
    hps = cfg.ret_heads_per_step
    n_hg = heads // hps
    nc = n_tok // chunk
    first, last = _block_flags(cfg.seq_lens, chunk)
    lg_f = jnp.log1p(-jnp.exp2(-dexp_f.astype(F32)))
    lg_b = jnp.log1p(-jnp.exp2(-dexp_b.astype(F32)))
    xi_f, zeta_f, cd_f = _ret_tables(lg_f, chunk, dk, backward=False)
    xi_b, zeta_b, cd_b = _ret_tables(lg_b, chunk, dk, backward=True)
    dmat = _ret_intra_decay(lg_f, lg_b, chunk, dk)
    k_col0 = heads * dk // (hps * dk)
    v_col0 = 2 * heads * dk // (hps * dv)
    g_col0 = v_col0 + n_hg

    def run(final, reset, cd, xi, zeta, extra_in, extra_specs, out_dtype):
        def tok(n):
            return (nc - 1 - n) if final else n

        tbl = pl.BlockSpec((hps, chunk, LANES), lambda hg, n, *_: (hg, 0, 0))
        in_specs = [pl.BlockSpec(memory_space=pltpu.SMEM),
                    pl.BlockSpec((chunk, hps * dk), lambda hg, n, *_: (tok(n), hg)),
                    pl.BlockSpec((chunk, hps * dk), lambda hg, n, *_: (tok(n), k_col0 + hg)),
                    pl.BlockSpec((chunk, hps * dv), lambda hg, n, *_: (tok(n), v_col0 + hg)),
                    tbl, tbl] + extra_specs(tok)
        kern = functools.partial(_ret_kernel, heads=hps, dk=dk, dv=dv, chunk=chunk, final=final)
        grid_spec = pltpu.PrefetchScalarGridSpec(
            num_scalar_prefetch=1,
            grid=(n_hg, nc),
            in_specs=in_specs,
            out_specs=pl.BlockSpec((chunk, hps * dv), lambda hg, n, *_: (tok(n), hg)),
            scratch_shapes=[pltpu.VMEM((hps, dk, dv), F32)],
        )
        return pl.pallas_call(
            kern,
            grid_spec=grid_spec,
            out_shape=jax.ShapeDtypeStruct((n_tok, heads * dv), out_dtype),
            compiler_params=_params("parallel", "arbitrary"),
        )(reset, cd, proj, proj, proj, xi, zeta, *extra_in)

    y1 = run(False, first, cd_f, xi_f, zeta_f, [dmat],
             lambda tok: [pl.BlockSpec((hps, chunk, chunk), lambda hg, n, *_: (hg, 0, 0))], F32)
    return run(True, last[::-1], cd_b, xi_b, zeta_b, [y1, proj],
               lambda tok: [pl.BlockSpec((chunk, hps * dv), lambda hg, n, *_: (tok(n), hg)),
                            pl.BlockSpec((chunk, hps * dv), lambda hg, n, *_: (tok(n), g_col0 + hg))],
               BF16)


def _combine_kernel(blk_start_ref, n_rows_ref, k_max_ref, ent_ref, x_ref, tok_cnt_ref, ye_ref, g_ref, *refs,
                    tb, cap, emit_x, split_blocks):
    refs = list(refs)
    sem_ref, buf_ref, xs_ref = refs.pop(), refs.pop(), refs.pop()
    x_out_ref = refs.pop(0) if emit_x else None
    r = pl.program_id(0)
    slot = r % 2

    cap_bits = cap.bit_length() - 1

    def start_rows(blk, to_slot):
        base = blk_start_ref[blk]
        n = n_rows_ref[blk]

        def start(ent):
            pltpu.make_async_copy(ye_ref.at[lax.shift_right_logical(ent, cap_bits)],
                                  buf_ref.at[to_slot, ent & (cap - 1)], sem_ref.at[to_slot]).start()

        def body(i, ents):
            following = ent_ref[base + 2 * i + 2], ent_ref[base + 2 * i + 3]
            start(ents[0])
            start(ents[1])
            return following
        last, _ = lax.fori_loop(0, n // 2, body, (ent_ref[base], ent_ref[base + 1]))

        @pl.when(n % 2 == 1)
        def _():
            start(last)

    @pl.when(r == 0)
    def _():
        start_rows(0, 0)

    @pl.when(r + 1 < pl.num_programs(0))
    def _():
        start_rows(r + 1, 1 - slot)

    def wait_rows(rows):
        def body(i, carry):
            pltpu.make_async_copy(ye_ref.at[pl.ds(0, rows)], buf_ref.at[slot, pl.ds(0, rows)],
                                  sem_ref.at[slot]).wait()
            return carry
        return body
    lax.fori_loop(0, n_rows_ref[r] // COMBINE_WAIT_ROWS, wait_rows(COMBINE_WAIT_ROWS), 0)
    lax.fori_loop(0, n_rows_ref[r] % COMBINE_WAIT_ROWS, wait_rows(1), 0)

    tok_cnt = tok_cnt_ref[...]
    k_max = k_max_ref[r]
    slabs = [slice(j * LANES, (j + 1) * LANES) for j in range(x_ref.shape[1] // LANES)]
    per_chunk = COMBINE_COLS // LANES
    sq = jnp.zeros((tb, LANES), F32)
    for c in range(0, len(slabs), per_chunk):
        chunk = slabs[c:c + per_chunk]

        def add_level(k, accs, chunk=chunk):
            live = k < tok_cnt
            level = pl.ds(pl.multiple_of(k * tb, tb), tb)
            return tuple(a + jnp.where(live, buf_ref[slot, level, s], 0.0) for a, s in zip(accs, chunk))
        sums = lax.fori_loop(0, k_max, add_level, tuple(x_ref[:, s] for s in chunk))
        for s, v in zip(chunk, sums):
            xs_ref[:, s] = v
            sq = sq + v * v
    if emit_x:
        x_out_ref[...] = xs_ref[...]
    inv = lax.rsqrt(jnp.sum(sq, axis=-1, keepdims=True) / x_ref.shape[1] + NORM_EPS)
    inv = jnp.broadcast_to(inv, (tb, LANES))

    def emit_normed(h_ref):
        for s in slabs:
            h_ref[:, s] = (xs_ref[:, s] * inv * g_ref[:, s]).astype(h_ref.dtype)

    if split_blocks is None:
        emit_normed(refs[0])
    else:
        @pl.when(r < split_blocks)
        def _():
            emit_normed(refs[0])

        @pl.when(r >= split_blocks)
        def _():
            emit_normed(refs[1])


def _token_histogram(tok, n_tok):
    lo_size = 256
    hi_size = pl.cdiv(n_tok, lo_size)
    hi = (tok[:, None] // lo_size == jnp.arange(hi_size, dtype=jnp.int32)[None, :]).astype(BF16)
    lo = (tok[:, None] % lo_size == jnp.arange(lo_size, dtype=jnp.int32)[None, :]).astype(BF16)
    hist = jnp.einsum("rh,rl->hl", hi, lo, preferred_element_type=F32)
    return hist.reshape(-1)[:n_tok].astype(jnp.int32)


def _combine(x, ye, idx, cfg, norm_g, last):
    n_tok, d = x.shape
    n_rows = idx.shape[0]
    tb = cfg.combine_rows
    max_k = cfg.n_experts
    cap = tb * max_k
    nblk = n_tok // tb
    pos = jnp.arange(n_rows, dtype=jnp.int32)
    tok_sorted, order = lax.sort((idx, pos), num_keys=1)
    new_tok = jnp.concatenate([jnp.ones((1,), bool), tok_sorted[1:] != tok_sorted[:-1]])
    rank = pos - lax.cummax(jnp.where(new_tok, pos, 0), axis=0)
    entries = order * cap + rank * tb + tok_sorted % tb
    entries = jnp.concatenate([entries, jnp.zeros((3,), jnp.int32)])
    tok_cnt = _token_histogram(idx, n_tok)
    blk_cnt = tok_cnt.reshape(nblk, tb)
    blk_rows = jnp.sum(blk_cnt, axis=1)
    blk_start = jnp.cumsum(blk_rows) - blk_rows
    k_max = jnp.max(blk_cnt, axis=1)

    rows_spec = pl.BlockSpec((tb, d), lambda r, *_: (r, 0))
    if last:
        n_a, n_b = cfg.group_tokens
        split = n_a // tb
        out_specs = [pl.BlockSpec((tb, d), lambda r, *_: (jnp.minimum(r, split - 1), 0)),
                     pl.BlockSpec((tb, d), lambda r, *_: (jnp.maximum(r - split, 0), 0))]
        out_shape = [jax.ShapeDtypeStruct((n_a, d), F32), jax.ShapeDtypeStruct((n_b, d), F32)]
    else:
        split = None
        out_specs = [rows_spec, rows_spec]
        out_shape = [jax.ShapeDtypeStruct((n_tok, d), F32), jax.ShapeDtypeStruct((n_tok, d), BF16)]
    grid_spec = pltpu.PrefetchScalarGridSpec(
        num_scalar_prefetch=4,
        grid=(nblk,),
        in_specs=[rows_spec,
                  pl.BlockSpec((tb, LANES), lambda r, *_: (r, 0)),
                  pl.BlockSpec(memory_space=pl.ANY),
                  pl.BlockSpec((1, d), lambda r, *_: (0, 0))],
        out_specs=out_specs,
        scratch_shapes=[pltpu.VMEM((tb, d), F32), pltpu.VMEM((2, cap, d), F32), pltpu.SemaphoreType.DMA((2,))],
    )
    return pl.pallas_call(
        functools.partial(_combine_kernel, tb=tb, cap=cap, emit_x=not last, split_blocks=split),
        grid_spec=grid_spec,
        out_shape=out_shape,
        compiler_params=_params("arbitrary", disable_bounds_checks=True),
    )(blk_start.astype(jnp.int32), blk_rows, k_max, entries, x, jnp.broadcast_to(tok_cnt[:, None], (n_tok, LANES)),
      ye, norm_g.reshape(1, d))


def _gather_norm_kernel(idx_ref, x_ref, g_ref, o_ref, buf_ref, sem_ref, *, rb):
    s = pl.program_id(0)
    slot = s % 2

    def start_rows(blk, to_slot):
        base = blk * rb

        def body(i, carry):
            pltpu.make_async_copy(x_ref.at[idx_ref[base + i]], buf_ref.at[to_slot, i], sem_ref.at[to_slot]).start()
            return carry
        lax.fori_loop(0, rb, body, 0, unroll=GATHER_UNROLL)

    @pl.when(s == 0)
    def _():
        start_rows(0, 0)

    @pl.when(s + 1 < pl.num_programs(0))
    def _():
        start_rows(s + 1, 1 - slot)

    def wait_rows(i, carry):
        pltpu.make_async_copy(x_ref.at[pl.ds(0, COMBINE_WAIT_ROWS)], buf_ref.at[slot, pl.ds(0, COMBINE_WAIT_ROWS)],
                              sem_ref.at[slot]).wait()
        return carry
    lax.fori_loop(0, rb // COMBINE_WAIT_ROWS, wait_rows, 0)
    o_ref[...] = _rmsnorm_f32(buf_ref[slot], g_ref[...]).astype(o_ref.dtype)


def _gather_norm(x, g, idx, cfg):
    n_rows = idx.shape[0]
    d = x.shape[1]
    rb = cfg.norm_rows
    grid_spec = pltpu.PrefetchScalarGridSpec(
        num_scalar_prefetch=1,
        grid=(n_rows // rb,),
        in_specs=[pl.BlockSpec(memory_space=pl.ANY),
                  pl.BlockSpec((1, d), lambda s, *_: (0, 0))],
        out_specs=pl.BlockSpec((rb, d), lambda s, *_: (s, 0)),
        scratch_shapes=[pltpu.VMEM((2, rb, d), F32), pltpu.SemaphoreType.DMA((2,))],
    )
    return pl.pallas_call(
        functools.partial(_gather_norm_kernel, rb=rb),
        grid_spec=grid_spec,
        out_shape=jax.ShapeDtypeStruct((n_rows, d), BF16),
        compiler_params=_params("arbitrary", disable_bounds_checks=True),
    )(idx, x, g.reshape(1, d))


def _expert_choice_ffn(x, norm_g, aff, w_gate, w_up, w_down, layer, cfg, next_norm_g, last):
    n_exp = cfg.n_experts
    d = x.shape[1]
    idx_all, gate_all = [], []
    start = 0
    for n_tok in cfg.group_tokens:
        cap = max(1, cfg.ec_capacity * n_tok // n_exp)
        gate, idx = lax.top_k(aff[:, start:start + n_tok], cap)
        idx_all.append(idx + start)
        gate_all.append(gate)
        start += n_tok
    caps = {i.shape[1] for i in idx_all}
    assert len(caps) == 1
    cap = caps.pop()
    idx = jnp.concatenate(idx_all, axis=0)
    gate = jnp.concatenate(gate_all, axis=0)
    xe = _gather_norm(x, norm_g, idx.reshape(-1), cfg).reshape(idx.shape + (d,))
    experts = dict(w_base=layer * n_exp, w_period=n_exp, bm=cap, bn=cfg.mm_cols)
    hid = _matmul(xe, w_gate, w2=w_up, out_dtype=BF16, **experts)
    ye = _matmul(hid, w_down, scale=gate[..., None], out_dtype=F32, **experts)
    return _combine(x, ye.reshape(-1, d), idx.reshape(-1), cfg, next_norm_g, last)


def _trunk(x_groups, cfg, norm_mix, norm_ffn, norm_final, w_attn_in, w_attn_out, attn_sink,
           w_ret_in, w_ret_out, dexp_f, dexp_b, w_router, w_exp_gate, w_exp_up, w_exp_down):
    depth = norm_mix.shape[0]
    w_exp_gate, w_exp_up, w_exp_down = (w.reshape((-1,) + w.shape[2:]) for w in (w_exp_gate, w_exp_up, w_exp_down))
    rows, cols = cfg.mm_rows, cfg.mm_cols
    h, x = _rmsnorm_concat(*x_groups, norm_mix[0], cfg.norm_rows)
    for i in range(depth):
        j = i // 2
        if i % 2 == 0:
            proj = _dense(h, w_attn_in[j].astype(BF16), out_dtype=BF16, bm=rows, bn=2 * cols)
            mixed = _window_attention(proj, attn_sink[j], cfg)
            x = _dense(mixed, w_attn_out[j].astype(BF16), out_dtype=F32, res=x, bm=rows // 2, bn=2 * cols)
        else:
            proj = _dense(h, w_ret_in[j].astype(BF16), out_dtype=BF16, bm=rows, bn=2 * cols)
            mixed = _retention(proj, dexp_f[j], dexp_b[j], cfg)
            x = _dense(mixed, w_ret_out[j].astype(BF16), out_dtype=F32, res=x, bm=rows // 2, bn=cols)
        aff = _rmsnorm_router(x, norm_ffn[i], w_router[i], cfg.norm_rows)
        last = i == depth - 1
        out = _expert_choice_ffn(x, norm_ffn[i], aff, w_exp_gate, w_exp_up, w_exp_down, i, cfg,
                                 norm_final if last else norm_mix[i + 1], last)
        if last:
            return out
        x, h = out


def _config(x_prompt, x_sample, att_heads, ret_heads, n_experts):
    b, t, d = x_prompt.shape
    db, dt, _ = x_sample.shape
    return Config(
        d_model=d, att_heads=att_heads, att_kv_heads=att_heads // 4, att_head_dim=d // att_heads,
        att_block=128, ret_heads=ret_heads, ret_dk=d // ret_heads, ret_dv=2 * d // ret_heads,
        ret_chunk=256, n_experts=n_experts, ec_capacity=2,
        seq_lens=(t,) * b + (dt,) * db, group_tokens=(b * t, db * dt),
        norm_rows=256, mm_rows=2048, mm_cols=256, ret_heads_per_step=4, combine_rows=64)


def kernel(x_prompt, x_sample, norm_mix, norm_ffn, norm_final, w_attn_in, w_attn_out, attn_sink,
           w_ret_in, w_ret_out, ret_decay_exp_fwd, ret_decay_exp_bwd,
           w_router, w_exp_gate, w_exp_up, w_exp_down):
    cfg = _config(x_prompt, x_sample, attn_sink.shape[1], ret_decay_exp_fwd.shape[1], w_router.shape[2])
    d = cfg.d_model
    x_groups = (x_prompt.reshape(-1, d), x_sample.reshape(-1, d))
    y_prompt, y_sample = _trunk(x_groups, cfg, norm_mix, norm_ffn, norm_final, w_attn_in, w_attn_out, attn_sink,
                                w_ret_in, w_ret_out, ret_decay_exp_fwd, ret_decay_exp_bwd,
                                w_router, w_exp_gate, w_exp_up, w_exp_down)
    return y_prompt.reshape(x_prompt.shape), y_sample.reshape(x_sample.shape)
```
